```python
import jax, jax.numpy as jnp
from jax import lax
import numpy as np

D_MODEL = 1024
BATCH = 4
SEQ = 8192
DEPTH = 4

N_MIXERS = 2
N_HEADS = 16
HEAD_DIM = D_MODEL // N_HEADS
Q_BLOCK = 128
CONV_WIDTH = 31
D_FF = 7 * D_MODEL // 2
N_EXPERTS = 8
TOP_K = 2
EPS = 1e-6
N_EVEN = (DEPTH + 1) // 2
N_ODD = DEPTH // 2

kernel_name = "fox_conformer_moe_hybrid"


def rmsnorm(x, g):
    xf = x.astype(jnp.float32)
    y = xf * lax.rsqrt(jnp.mean(xf * xf, axis=-1, keepdims=True) + EPS)
    return (y * g.astype(jnp.float32)).astype(x.dtype)


def layernorm(x, g, b):
    xf = x.astype(jnp.float32)
    mu = jnp.mean(xf, axis=-1, keepdims=True)
    xc = xf - mu
    y = xc * lax.rsqrt(jnp.mean(xc * xc, axis=-1, keepdims=True) + EPS)
    return (y * g.astype(jnp.float32) + b.astype(jnp.float32)).astype(x.dtype)


def forgetting_attention(h, w_qkv, w_f, b_f, w_o):
    B, S, _ = h.shape
    q, k, v = jnp.split(h @ w_qkv, 3, axis=-1)
    to_heads = lambda t: t.reshape(B, S, N_HEADS, HEAD_DIM).transpose(0, 2, 1, 3)
    q, k, v = to_heads(q), to_heads(k), to_heads(v)
    log_f = jax.nn.log_sigmoid((h @ w_f + b_f).astype(jnp.float32))
    c = jnp.cumsum(log_f, axis=1).transpose(0, 2, 1)
    nb = S // Q_BLOCK
    qb = q.reshape(B, N_HEADS, nb, Q_BLOCK, HEAD_DIM).transpose(2, 0, 1, 3, 4)
    cb = c.reshape(B, N_HEADS, nb, Q_BLOCK).transpose(2, 0, 1, 3)
    starts = jnp.arange(nb, dtype=jnp.int32) * Q_BLOCK
    k_pos = jnp.arange(S, dtype=jnp.int32)
    scale = HEAD_DIM ** -0.5

    def one_block(args):
        q_i, c_i, start = args
        s = jnp.einsum('bhqd,bhkd->bhqk', q_i, k, preferred_element_type=jnp.float32) * scale
        s = s + (c_i[..., :, None] - c[..., None, :])
        q_pos = start + jnp.arange(Q_BLOCK, dtype=jnp.int32)
        s = jnp.where(k_pos[None, :] <= q_pos[:, None], s, -jnp.inf)
        p = jax.nn.softmax(s, axis=-1)
        return jnp.einsum('bhqk,bhkd->bhqd', p.astype(v.dtype), v)

    o = lax.map(one_block, (qb, cb, starts))
    o = o.transpose(1, 0, 3, 2, 4).reshape(B, S, D_MODEL)
    return o @ w_o


def conformer_conv(h, w_pw1, b_pw1, w_dw, b_dw, ln_g, ln_b, w_pw2, b_pw2):
    u = jax.nn.glu(h @ w_pw1 + b_pw1, axis=-1)
    u = lax.conv_general_dilated(
        u, w_dw[:, None, :].astype(u.dtype), window_strides=(1,),
        padding=[(CONV_WIDTH - 1, 0)],
        dimension_numbers=('NWC', 'WIO', 'NWC'),
        feature_group_count=D_MODEL) + b_dw
    u = jax.nn.silu(layernorm(u, ln_g, ln_b))
    return u @ w_pw2 + b_pw2


def swiglu(h, w_gate, w_up, w_down):
    return (jax.nn.silu(h @ w_gate) * (h @ w_up)) @ w_down


def moe_swiglu(h, w_router, w_gate, w_up, w_down):
    B, S, D = h.shape
    t = h.reshape(B * S, D)
    logits = (t @ w_router).astype(jnp.float32)
    top_v, top_i = lax.top_k(logits, TOP_K)
    gates = jax.nn.softmax(top_v, axis=-1)
    combine = jnp.sum(jax.nn.one_hot(top_i, N_EXPERTS, dtype=jnp.float32) * gates[..., None], axis=1)
    combine = combine.astype(t.dtype)
    out = jnp.zeros_like(t)
    for e in range(N_EXPERTS):
        out = out + combine[:, e:e + 1] * swiglu(t, w_gate[e], w_up[e], w_down[e])
    return out.reshape(B, S, D)


def setup_inputs(seed: int = 0) -> dict:
    key = jax.random.key(seed)
    ks = jax.random.split(key, 32)
    D, F, E, H, W = D_MODEL, D_FF, N_EXPERTS, N_HEADS, CONV_WIDTH
    nrm = lambda k, shape, fan_in: jax.random.normal(k, shape, jnp.float32) * (fan_in ** -0.5)
    gain = lambda k, shape: 1.0 + 0.02 * jax.random.normal(k, shape, jnp.float32)
    bias = lambda k, shape: 0.02 * jax.random.normal(k, shape, jnp.float32)
    return {
        "x": jax.random.normal(ks[0], (BATCH, SEQ, D), jnp.float32),
        "attn_norm": gain(ks[1], (N_EVEN, D)),
        "w_qkv": nrm(ks[2], (N_EVEN, D, 3 * D), D),
        "w_fgate": nrm(ks[3], (N_EVEN, D, H), D),
        "b_fgate": jax.random.uniform(ks[4], (N_EVEN, H), jnp.float32, 1.0, 4.0),
        "w_attn_out": nrm(ks[5], (N_EVEN, D, D), D),
        "conv_norm": gain(ks[6], (N_ODD, D)),
        "w_pw1": nrm(ks[7], (N_ODD, D, 2 * D), D),
        "b_pw1": bias(ks[8], (N_ODD, 2 * D)),
        "w_dw": nrm(ks[9], (N_ODD, W, D), W),
        "b_dw": bias(ks[10], (N_ODD, D)),
        "conv_ln_g": gain(ks[11], (N_ODD, D)),
        "conv_ln_b": bias(ks[12], (N_ODD, D)),
        "w_pw2": nrm(ks[13], (N_ODD, D, D), D),
        "b_pw2": bias(ks[14], (N_ODD, D)),
        "dense_norm": gain(ks[15], (N_EVEN, D)),
        "w_dense_gate": nrm(ks[16], (N_EVEN, D, F), D),
        "w_dense_up": nrm(ks[17], (N_EVEN, D, F), D),
        "w_dense_down": nrm(ks[18], (N_EVEN, F, D), F),
        "moe_norm": gain(ks[19], (N_ODD, D)),
        "w_router": nrm(ks[20], (N_ODD, D, E), D),
        "w_exp_gate": nrm(ks[21], (N_ODD, E, D, F), D),
        "w_exp_up": nrm(ks[22], (N_ODD, E, D, F), D),
        "w_exp_down": nrm(ks[23], (N_ODD, E, F, D), F),
        "final_norm": gain(ks[24], (D,)),
    }


def reference(x, attn_norm, w_qkv, w_fgate, b_fgate, w_attn_out,
              conv_norm, w_pw1, b_pw1, w_dw, b_dw, conv_ln_g, conv_ln_b, w_pw2, b_pw2,
              dense_norm, w_dense_gate, w_dense_up, w_dense_down,
              moe_norm, w_router, w_exp_gate, w_exp_up, w_exp_down,
              final_norm):
    for i in range(DEPTH):
        j = i // N_MIXERS
        if i % N_MIXERS == 0:
            x = x + forgetting_attention(rmsnorm(x, attn_norm[j]), w_qkv[j], w_fgate[j],
                                         b_fgate[j], w_attn_out[j])
            x = x + swiglu(rmsnorm(x, dense_norm[j]), w_dense_gate[j], w_dense_up[j], w_dense_down[j])
        else:
            x = x + conformer_conv(rmsnorm(x, conv_norm[j]), w_pw1[j], b_pw1[j], w_dw[j], b_dw[j],
                                   conv_ln_g[j], conv_ln_b[j], w_pw2[j], b_pw2[j])
            x = x + moe_swiglu(rmsnorm(x, moe_norm[j]), w_router[j], w_exp_gate[j],
                               w_exp_up[j], w_exp_down[j])
    return rmsnorm(x, final_norm)
```

```python
import functools

import numpy as np
import jax
import jax.numpy as jnp
from jax import lax
from jax.experimental import pallas as pl
from jax.experimental.pallas import tpu as pltpu

N_HEADS = 16
HEAD_DIM = 64
CONV_WIDTH = 31
N_EXPERTS = 8
EPS = 1e-6

LANES = 128
HEAD_LANES = 128
CONV_HALO = 32
VMEM_LIMIT = 56 * 1024 * 1024
MASK_VALUE = -1e30

F32 = jnp.float32
BF16 = jnp.bfloat16
I32 = jnp.int32


def _params(*sem):
    return pltpu.CompilerParams(dimension_semantics=sem, vmem_limit_bytes=VMEM_LIMIT)


def _rms(x, g):
    return x * lax.rsqrt(jnp.mean(x * x, axis=-1, keepdims=True) + EPS) * g


def _sigmoid(x):
    return 1.0 / (1.0 + jnp.exp(-x))


def _dot(a, b):
    return jnp.dot(a, b, preferred_element_type=F32)


def _split3(x):
    hi = x.astype(BF16)
    r1 = x - hi.astype(F32)
    mid = r1.astype(BF16)
    lo = (r1 - mid.astype(F32)).astype(BF16)
    return hi, mid, lo


def _decay_placement():
    eq = np.zeros((LANES, N_HEADS * HEAD_LANES), np.float32)
    ek = np.zeros((LANES, N_HEADS * HEAD_LANES), np.float32)
    for h in range(N_HEADS):
        base = h * HEAD_LANES + HEAD_DIM
        for r in range(3):
            eq[r * N_HEADS + h, base + r] = 1.0
            eq[3 * N_HEADS, base + 3 + r] = 1.0
            ek[3 * N_HEADS, base + r] = 1.0
            ek[r * N_HEADS + h, base + 3 + r] = -1.0
    return jnp.asarray(eq, BF16), jnp.asarray(ek, BF16)


def _attn_pre_kernel(x_ref, g_ref, wqkv_ref, wf_ref, bf_ref, eq_ref, ek_ref,
                     qa_ref, ka_ref, v_ref, carry_ref, *, tiles_per_seq):
    i = pl.program_id(0)
    tm, d = x_ref.shape

    @pl.when(i % tiles_per_seq == 0)
    def _():
        carry_ref[...] = jnp.zeros_like(carry_ref)

    hb = _rms(x_ref[...], g_ref[...]).astype(BF16)

    z = _dot(hb, wf_ref[...]) + bf_ref[...]
    lf = jnp.minimum(z, 0.0) - jnp.log(1.0 + jnp.exp(-jnp.abs(z)))
    lane = lax.broadcasted_iota(I32, (tm, LANES), 1)
    lf = jnp.where(lane < N_HEADS, lf, 0.0)
    row = lax.broadcasted_iota(I32, (tm, tm), 0)
    col = lax.broadcasted_iota(I32, (tm, tm), 1)
    tri = jnp.where(col <= row, 1.0, 0.0).astype(BF16)
    hi, mid, lo = _split3(lf)
    c = _dot(tri, hi) + _dot(tri, mid) + _dot(tri, lo) + carry_ref[...]
    carry_ref[...] = c[tm - 1:tm, :]

    chi, cmid, clo = _split3(c)
    packed = (chi.astype(F32) + pltpu.roll(cmid.astype(F32), N_HEADS, 1)
              + pltpu.roll(clo.astype(F32), 2 * N_HEADS, 1)
              + jnp.where(lane == 3 * N_HEADS, 1.0, 0.0)).astype(BF16)
    augq = _dot(packed, eq_ref[...])
    augk = _dot(packed, ek_ref[...])

    low = lane < HEAD_DIM
    scale = HEAD_DIM ** -0.5

    def emit(dst_ref, chunk, aug, head0, mul):
        for pp in range(chunk.shape[1] // LANES):
            pair = chunk[:, pp * LANES:(pp + 1) * LANES]
            if mul != 1.0:
                pair = pair * mul
            h0 = head0 + 2 * pp
            first = jnp.where(low, pair, aug[:, h0 * HEAD_LANES:(h0 + 1) * HEAD_LANES])
            second = jnp.where(low, pltpu.roll(pair, HEAD_DIM, 1),
                               aug[:, (h0 + 1) * HEAD_LANES:(h0 + 2) * HEAD_LANES])
            dst_ref[:, h0 * HEAD_LANES:(h0 + 1) * HEAD_LANES] = first.astype(BF16)
            dst_ref[:, (h0 + 1) * HEAD_LANES:(h0 + 2) * HEAD_LANES] = second.astype(BF16)

    cw = 256
    for cc in range(d // cw):
        heads0 = cc * (cw // HEAD_DIM)
        emit(qa_ref, _dot(hb, wqkv_ref[:, cc * cw:(cc + 1) * cw]), augq, heads0, scale)
        emit(ka_ref, _dot(hb, wqkv_ref[:, d + cc * cw:d + (cc + 1) * cw]), augk, heads0, 1.0)
        v_ref[:, cc * cw:(cc + 1) * cw] = _dot(hb, wqkv_ref[:, 2 * d + cc * cw:2 * d + (cc + 1) * cw]).astype(BF16)


def attn_pre(x, g, wqkv, wf, bf, seq, tm=512):
    t, d = x.shape
    tm = min(tm, seq)
    eq, ek = _decay_placement()
    wf_p = jnp.zeros((d, LANES), BF16).at[:, :N_HEADS].set(wf.astype(BF16))
    bf_p = jnp.zeros((1, LANES), F32).at[0, :N_HEADS].set(bf)
    na = N_HEADS * HEAD_LANES
    full = lambda shape: pl.BlockSpec(shape, lambda i: (0, 0))
    return pl.pallas_call(
        functools.partial(_attn_pre_kernel, tiles_per_seq=seq // tm),
        grid=(t // tm,),
        in_specs=[pl.BlockSpec((tm, d), lambda i: (i, 0)), full((1, d)), full((d, 3 * d)),
                  full((d, LANES)), full((1, LANES)), full((LANES, na)), full((LANES, na))],
        out_specs=[pl.BlockSpec((tm, na), lambda i: (i, 0)), pl.BlockSpec((tm, na), lambda i: (i, 0)),
                   pl.BlockSpec((tm, d), lambda i: (i, 0))],
        out_shape=[jax.ShapeDtypeStruct((t, na), BF16), jax.ShapeDtypeStruct((t, na), BF16),
                   jax.ShapeDtypeStruct((t, d), BF16)],
        scratch_shapes=[pltpu.VMEM((1, LANES), F32)],
        compiler_params=_params("arbitrary"),
        name="attn_pre",
    )(x, g.reshape(1, d), wqkv.astype(BF16), wf_p, bf_p, eq, ek)


def _flash_kernel(qa_ref, ka_ref, v_ref, o_ref, m_ref, l_ref, acc_ref, *, tq):
    qi = pl.program_id(2)
    row = lax.broadcasted_iota(I32, (tq, tq), 0)
    col = lax.broadcasted_iota(I32, (tq, tq), 1)
    causal = col <= row
    outs = []
    for hh in range(2):
        q = qa_ref[:, hh * HEAD_LANES:(hh + 1) * HEAD_LANES]
        m_ref[...] = jnp.full_like(m_ref, MASK_VALUE)
        l_ref[...] = jnp.zeros_like(l_ref)
        acc_ref[...] = jnp.zeros_like(acc_ref)

        def step(j, masked):
            start = pl.multiple_of(j * tq, tq)
            k = ka_ref[pl.ds(start, tq), hh * HEAD_LANES:(hh + 1) * HEAD_LANES]
            s = lax.dot_general(q, k, (((1,), (1,)), ((), ())), preferred_element_type=F32)
            if masked:
                s = jnp.where(causal, s, MASK_VALUE)
            m_prev = m_ref[...]
            m_new = jnp.maximum(m_prev, jnp.max(s, axis=1, keepdims=True))
            alpha = jnp.exp(m_prev - m_new)
            p = jnp.exp(s - m_new)
            l_ref[...] = alpha * l_ref[...] + jnp.sum(p, axis=1, keepdims=True)
            acc_ref[...] = alpha * acc_ref[...] + _dot(p.astype(BF16), v_ref[pl.ds(start, tq), :])
            m_ref[...] = m_new

        def body(j, carry):
            step(j, False)
            return carry

        lax.fori_loop(0, qi, body, 0)
        step(qi, True)
        outs.append(acc_ref[...] / l_ref[...])
    lane = lax.broadcasted_iota(I32, (tq, LANES), 1)
    o_ref[...] = jnp.where(lane < HEAD_DIM, outs[0], outs[1]).astype(BF16)


def flash(qa, ka, v, batch, seq, tq=256):
    t, d = v.shape
    tq = min(tq, seq)
    nq = seq // tq
    pairs = N_HEADS // 2
    return pl.pallas_call(
        functools.partial(_flash_kernel, tq=tq),
        grid=(batch, pairs, nq),
        in_specs=[pl.BlockSpec((tq, 2 * HEAD_LANES), lambda b, p, q: (b * nq + q, p)),
                  pl.BlockSpec((seq, 2 * HEAD_LANES), lambda b, p, q: (b, p)),
                  pl.BlockSpec((seq, LANES), lambda b, p, q: (b, p))],
        out_specs=pl.BlockSpec((tq, LANES), lambda b, p, q: (b * nq + q, p)),
        out_shape=jax.ShapeDtypeStruct((t, d), BF16),
        scratch_shapes=[pltpu.VMEM((tq, 1), F32), pltpu.VMEM((tq, 1), F32), pltpu.VMEM((tq, LANES), F32)],
        compiler_params=_params("arbitrary", "arbitrary", "arbitrary"),
        name="flash",
    )(qa, ka, v)


def _ffn_kernel(x_ref, o_ref, wo_ref, g_ref, wg_ref, wu_ref, wd_ref, out_ref, hb_ref):
    f = pl.program_id(1)

    @pl.when(f == 0)
    def _():
        x1 = x_ref[...] + _dot(o_ref[...], wo_ref[...])
        out_ref[...] = x1
        hb_ref[...] = _rms(x1, g_ref[...]).astype(BF16)

    hb = hb_ref[...]
    a = _dot(hb, wg_ref[...])
    u = _dot(hb, wu_ref[...])
    act = (a * _sigmoid(a) * u).astype(BF16)
    out_ref[...] += _dot(act, wd_ref[...])


def ffn_dense(x, o, wo, g, wg, wu, wd, tm=1024, tf=512):
    t, d = x.shape
    ff = wg.shape[1]
    tm = min(tm, t)
    tf = min(tf, ff)
    return pl.pallas_call(
        _ffn_kernel,
        grid=(t // tm, ff // tf),
        in_specs=[pl.BlockSpec((tm, d), lambda i, f: (i, 0)), pl.BlockSpec((tm, d), lambda i, f: (i, 0)),
                  pl.BlockSpec((d, d), lambda i, f: (0, 0)), pl.BlockSpec((1, d), lambda i, f: (0, 0)),
                  pl.BlockSpec((d, tf), lambda i, f: (0, f)), pl.BlockSpec((d, tf), lambda i, f: (0, f)),
                  pl.BlockSpec((tf, d), lambda i, f: (f, 0))],
        out_specs=pl.BlockSpec((tm, d), lambda i, f: (i, 0)),
        out_shape=jax.ShapeDtypeStruct((t, d), F32),
        scratch_shapes=[pltpu.VMEM((tm, d), BF16)],
        compiler_params=_params("arbitrary", "arbitrary"),
        name="ffn_dense",
    )(x, o, wo.astype(BF16), g.reshape(1, d), wg.astype(BF16), wu.astype(BF16), wd.astype(BF16))


def _conv_pre_kernel(x_ref, g_ref, w1_ref, b1_ref, u_ref):
    d = x_ref.shape[1]
    hb = _rms(x_ref[...], g_ref[...]).astype(BF16)
    cw = 256
    for cc in range(d // cw):
        lo, hi = cc * cw, (cc + 1) * cw
        a = _dot(hb, w1_ref[:, lo:hi]) + b1_ref[:, lo:hi]
        b = _dot(hb, w1_ref[:, d + lo:d + hi]) + b1_ref[:, d + lo:d + hi]
        u_ref[:, lo:hi] = a * _sigmoid(b)


def conv_pre(x, g, w1, b1, tm=512):
    t, d = x.shape
    tm = min(tm, t)
    return pl.pallas_call(
        _conv_pre_kernel,
        grid=(t // tm,),
        in_specs=[pl.BlockSpec((tm, d), lambda i: (i, 0)), pl.BlockSpec((1, d), lambda i: (0, 0)),
                  pl.BlockSpec((d, 2 * d), lambda i: (0, 0)), pl.BlockSpec((1, 2 * d), lambda i: (0, 0))],
        out_specs=pl.BlockSpec((tm, d), lambda i: (i, 0)),
        out_shape=jax.ShapeDtypeStruct((t, d), F32),
        compiler_params=_params("arbitrary"),
        name="conv_pre",
    )(x, g.reshape(1, d), w1.astype(BF16), b1.reshape(1, 2 * d))


def _conv_post_kernel(x_ref, u_ref, halo_ref, wdw_ref, bdw_ref, lng_ref, lnb_ref, w2_ref, b2_ref,
                      out_ref, ext_ref, y_ref, *, tiles_per_seq):
    i = pl.program_id(0)
    tm, d = x_ref.shape
    ext_ref[CONV_HALO:CONV_HALO + tm, :] = u_ref[...]

    @pl.when(i % tiles_per_seq == 0)
    def _():
        ext_ref[0:CONV_HALO, :] = jnp.zeros((CONV_HALO, d), F32)

    @pl.when(i % tiles_per_seq != 0)
    def _():
        ext_ref[0:CONV_HALO, :] = halo_ref[...]

    shift = CONV_HALO - (CONV_WIDTH - 1)
    rw, cw = min(64, tm), 256
    for cc in range(d // cw):
        cols = slice(cc * cw, (cc + 1) * cw)
        for rr in range(tm // rw):
            acc = jnp.zeros((rw, cw), F32)
            for j in range(CONV_WIDTH):
                r0 = rr * rw + shift + j
                acc = acc + wdw_ref[j:j + 1, cols] * ext_ref[r0:r0 + rw, cols]
            y_ref[rr * rw:(rr + 1) * rw, cols] = acc + bdw_ref[:, cols]

    y = y_ref[...]
    mu = jnp.mean(y, axis=-1, keepdims=True)
    yc = y - mu
    z = yc * lax.rsqrt(jnp.mean(yc * yc, axis=-1, keepdims=True) + EPS) * lng_ref[...] + lnb_ref[...]
    zb = (z * _sigmoid(z)).astype(BF16)
    out_ref[...] = x_ref[...] + _dot(zb, w2_ref[...]) + b2_ref[...]


def conv_post(x, u, wdw, bdw, lng, lnb, w2, b2, seq, tm=256):
    t, d = x.shape
    tm = min(tm, seq)
    hb = tm // CONV_HALO
    row = lambda a: a.reshape(1, d)
    return pl.pallas_call(
        functools.partial(_conv_post_kernel, tiles_per_seq=seq // tm),
        grid=(t // tm,),
        in_specs=[pl.BlockSpec((tm, d), lambda i: (i, 0)), pl.BlockSpec((tm, d), lambda i: (i, 0)),
                  pl.BlockSpec((CONV_HALO, d), lambda i: (jnp.maximum(i * hb - 1, 0), 0)),
                  pl.BlockSpec((CONV_WIDTH, d), lambda i: (0, 0)), pl.BlockSpec((1, d), lambda i: (0, 0)),
                  pl.BlockSpec((1, d), lambda i: (0, 0)), pl.BlockSpec((1, d), lambda i: (0, 0)),
                  pl.BlockSpec((d, d), lambda i: (0, 0)), pl.BlockSpec((1, d), lambda i: (0, 0))],
        out_specs=pl.BlockSpec((tm, d), lambda i: (i, 0)),
        out_shape=jax.ShapeDtypeStruct((t, d), F32),
        scratch_shapes=[pltpu.VMEM((tm + CONV_HALO, d), F32), pltpu.VMEM((tm, d), F32)],
        compiler_params=_params("arbitrary"),
        name="conv_post",
    )(x, u, u, wdw, row(bdw), row(lng), row(lnb), w2.astype(BF16), row(b2))


def _router_kernel(x_ref, g_ref, whi_ref, wlo_ref, slots_ref, gates_ref, counts_ref, carry_ref):
    i = pl.program_id(0)
    tm = x_ref.shape[0]

    @pl.when(i == 0)
    def _():
        carry_ref[...] = jnp.zeros_like(carry_ref)

    h = _rms(x_ref[...], g_ref[...])
    h_hi = h.astype(BF16)
    h_lo = (h - h_hi.astype(F32)).astype(BF16)
    logits = _dot(h_hi, whi_ref[...]) + _dot(h_lo, whi_ref[...]) + _dot(h_hi, wlo_ref[...])
    lane = lax.broadcasted_iota(I32, (tm, LANES), 1)
    lg = jnp.where(lane < N_EXPERTS, logits, -jnp.inf)
    m1 = jnp.max(lg, axis=1, keepdims=True)
    i1 = jnp.min(jnp.where(lg == m1, lane, LANES), axis=1, keepdims=True)
    lg2 = jnp.where(lane == i1, -jnp.inf, lg)
    m2 = jnp.max(lg2, axis=1, keepdims=True)
    i2 = jnp.min(jnp.where(lg2 == m2, lane, LANES), axis=1, keepdims=True)
    e2 = jnp.exp(m2 - m1)
    g0 = 1.0 / (1.0 + e2)
    g1 = e2 / (1.0 + e2)

    oh0 = lane == i1
    oh1 = lane == i2
    member = jnp.where(oh0 | oh1, 1.0, 0.0)
    row = lax.broadcasted_iota(I32, (tm, tm), 0)
    col = lax.broadcasted_iota(I32, (tm, tm), 1)
    before = jnp.where(col < row, 1.0, 0.0).astype(BF16)
    rank = _dot(before, member.astype(BF16)) + carry_ref[...]
    r0 = jnp.sum(jnp.where(oh0, rank, 0.0), axis=1, keepdims=True).astype(I32)
    r1 = jnp.sum(jnp.where(oh1, rank, 0.0), axis=1, keepdims=True).astype(I32)
    carry_ref[...] += jnp.sum(member, axis=0, keepdims=True)
    counts_ref[...] = carry_ref[...]
    slots_ref[...] = jnp.where(lane == 0, r0, jnp.where(lane == 1, r1, jnp.where(lane == 2, i1, jnp.where(lane == 3, i2, 0))))
    gates_ref[...] = jnp.where(lane == 0, g0, jnp.where(lane == 1, g1, 0.0))


def router(x, g, wr, tm=512):
    t, d = x.shape
    tm = min(tm, t)
    wr_p = jnp.zeros((d, LANES), F32).at[:, :N_EXPERTS].set(wr)
    w_hi = wr_p.astype(BF16)
    w_lo = (wr_p - w_hi.astype(F32)).astype(BF16)
    return pl.pallas_call(
        _router_kernel,
        grid=(t // tm,),
        in_specs=[pl.BlockSpec((tm, d), lambda i: (i, 0)), pl.BlockSpec((1, d), lambda i: (0, 0)),
                  pl.BlockSpec((d, LANES), lambda i: (0, 0)), pl.BlockSpec((d, LANES), lambda i: (0, 0))],
        out_specs=[pl.BlockSpec((tm, LANES), lambda i: (i, 0)), pl.BlockSpec((tm, LANES), lambda i: (i, 0)),
                   pl.BlockSpec((1, LANES), lambda i: (0, 0))],
        out_shape=[jax.ShapeDtypeStruct((t, LANES), I32), jax.ShapeDtypeStruct((t, LANES), F32),
                   jax.ShapeDtypeStruct((1, LANES), F32)],
        scratch_shapes=[pltpu.VMEM((1, LANES), F32)],
        compiler_params=_params("arbitrary"),
        name="router",
    )(x, g.reshape(1, d), w_hi, w_lo)


def _dispatch_kernel(fill_ref, slots_ref, x_ref, xs_ref, zero_ref, sem, *, tmf):
    i = pl.program_id(0)
    tm, d = x_ref.shape

    def body(r, carry):
        for k in range(2):
            s = slots_ref[0, 0, 2 * r + k]
            pltpu.make_async_copy(x_ref.at[pl.ds(r, 1), :], xs_ref.at[pl.ds(s, 1), :], sem).start()
        return carry

    lax.fori_loop(0, tm, body, 0)
    for k in range(2):
        pltpu.make_async_copy(x_ref, xs_ref.at[pl.ds(0, tm), :], sem).wait()

    @pl.when(i == pl.num_programs(0) - 1)
    def _():
        zero_ref[...] = jnp.zeros_like(zero_ref)
        sizes = []
        size = tmf // 2
        while size >= 8:
            sizes.append(size)
            size //= 2

        def zero_row(r, carry):
            row_copy = pltpu.make_async_copy(zero_ref.at[pl.ds(0, 1), :], xs_ref.at[pl.ds(r, 1), :], sem)
            row_copy.start()
            row_copy.wait()
            return carry

        def zero_tile(n, carry):
            tile_copy = pltpu.make_async_copy(zero_ref, xs_ref.at[pl.ds(pl.multiple_of(n * tmf, tmf), tmf), :], sem)
            tile_copy.start()
            tile_copy.wait()
            return carry

        for e in range(N_EXPERTS):
            end = fill_ref[e]
            end8 = (end + 7) // 8 * 8
            lax.fori_loop(end, end8, zero_row, 0)
            rem = fill_ref[N_EXPERTS + e] - end8
            pos = end8
            for size in sizes:
                chunk = pltpu.make_async_copy(zero_ref.at[pl.ds(0, size), :],
                                              xs_ref.at[pl.ds(pl.multiple_of(pos, 8), size), :], sem)
                take = (rem & size) != 0

                @pl.when(take)
                def _():
                    chunk.start()
                    chunk.wait()

                pos = pos + jnp.where(take, size, 0)
        lax.fori_loop(fill_ref[2 * N_EXPERTS], fill_ref[2 * N_EXPERTS + 1], zero_tile, 0)


def dispatch(x, slots_flat, fill, rows, tmf, tm=256):
    t, d = x.shape
    tm = min(tm, t)
    nt = t // tm
    return pl.pallas_call(
        functools.partial(_dispatch_kernel, tmf=tmf),
        grid_spec=pltpu.PrefetchScalarGridSpec(
            num_scalar_prefetch=1,
            grid=(nt,),
            in_specs=[pl.BlockSpec((1, 1, 2 * tm), lambda i, c: (i, 0, 0), memory_space=pltpu.SMEM),
                      pl.BlockSpec((tm, d), lambda i, c: (i, 0))],
            out_specs=pl.BlockSpec(memory_space=pl.ANY),
            scratch_shapes=[pltpu.VMEM((tmf, d), F32), pltpu.SemaphoreType.DMA(())],
        ),
        out_shape=jax.ShapeDtypeStruct((rows, d), F32),
        compiler_params=_params("arbitrary"),
        name="dispatch",
    )(fill, slots_flat.reshape(nt, 1, 2 * tm), x)


def _moe_ffn_kernel(te_ref, na_ref, xs_ref, g_ref, wg_ref, wu_ref, wd_ref, ys_ref, hb_ref):
    i = pl.program_id(0)
    f = pl.program_id(1)

    @pl.when(f == 0)
    def _():
        ys_ref[...] = jnp.zeros_like(ys_ref)

    @pl.when(i < na_ref[0])
    def _():
        @pl.when(f == 0)
        def _():
            hb_ref[...] = _rms(xs_ref[...], g_ref[...]).astype(BF16)

        hb = hb_ref[...]
        a = _dot(hb, wg_ref[...])
        u = _dot(hb, wu_ref[...])
        act = (a * _sigmoid(a) * u).astype(BF16)
        ys_ref[...] += _dot(act, wd_ref[...])


def moe_ffn(xs, g, wg, wu, wd, tile_e, n_act, tmf, tf=512):
    rows, d = xs.shape
    ff = wg.shape[2]
    tf = min(tf, ff)
    nf = ff // tf

    def live(i, na):
        return jnp.minimum(i, na[0] - 1)

    def fidx(i, f, na):
        return jnp.where(i < na[0], f, nf - 1)

    return pl.pallas_call(
        _moe_ffn_kernel,
        grid_spec=pltpu.PrefetchScalarGridSpec(
            num_scalar_prefetch=2,
            grid=(rows // tmf, nf),
            in_specs=[pl.BlockSpec((tmf, d), lambda i, f, te, na: (live(i, na), 0)),
                      pl.BlockSpec((1, d), lambda i, f, te, na: (0, 0)),
                      pl.BlockSpec((None, d, tf), lambda i, f, te, na: (te[live(i, na)], 0, fidx(i, f, na))),
                      pl.BlockSpec((None, d, tf), lambda i, f, te, na: (te[live(i, na)], 0, fidx(i, f, na))),
                      pl.BlockSpec((None, tf, d), lambda i, f, te, na: (te[live(i, na)], fidx(i, f, na), 0))],
            out_specs=pl.BlockSpec((tmf, d), lambda i, f, te, na: (i, 0)),
            scratch_shapes=[pltpu.VMEM((tmf, d), BF16)],
        ),
        out_shape=jax.ShapeDtypeStruct((rows, d), F32),
        compiler_params=_params("arbitrary", "arbitrary"),
        name="moe_ffn",
    )(tile_e, n_act, xs, g.reshape(1, d), wg.astype(BF16), wu.astype(BF16), wd.astype(BF16))


def _combine_kernel(slots_ref, x_ref, gates_ref, ys_ref, gf_ref, out_ref, ybuf_ref, sem, *, final_norm):
    tm, d = x_ref.shape

    def body(r, carry):
        for k in range(2):
            s = slots_ref[0, 0, 2 * r + k]
            pltpu.make_async_copy(ys_ref.at[pl.ds(s, 1), :], ybuf_ref.at[k, pl.ds(r, 1), :], sem).start()
        return carry

    lax.fori_loop(0, tm, body, 0)
    for k in range(2):
        pltpu.make_async_copy(ys_ref.at[pl.ds(0, tm), :], ybuf_ref.at[k], sem).wait()

    gates = gates_ref[...]
    out = x_ref[...] + gates[:, 0:1] * ybuf_ref[0] + gates[:, 1:2] * ybuf_ref[1]
    if final_norm:
        out = _rms(out, gf_ref[...])
    out_ref[...] = out


def combine(x, slots_flat, gates, ys, g_final, final_norm, tm=256):
    t, d = x.shape
    tm = min(tm, t)
    nt = t // tm
    return pl.pallas_call(
        functools.partial(_combine_kernel, final_norm=final_norm),
        grid=(nt,),
        in_specs=[pl.BlockSpec((1, 1, 2 * tm), lambda i: (i, 0, 0), memory_space=pltpu.SMEM),
                  pl.BlockSpec((tm, d), lambda i: (i, 0)), pl.BlockSpec((tm, LANES), lambda i: (i, 0)),
                  pl.BlockSpec(memory_space=pl.ANY), pl.BlockSpec((1, d), lambda i: (0, 0))],
        out_specs=pl.BlockSpec((tm, d), lambda i: (i, 0)),
        out_shape=jax.ShapeDtypeStruct((t, d), F32),
        scratch_shapes=[pltpu.VMEM((2, tm, d), F32), pltpu.SemaphoreType.DMA(())],
        compiler_params=_params("arbitrary"),
        name="combine",
    )(slots_flat.reshape(nt, 1, 2 * tm), x, gates, ys, g_final.reshape(1, d))


def moe_layer(x, g, wr, wg, wu, wd, g_final, final_norm, tmf=1024):
    t, d = x.shape
    tmf = min(tmf, t)
    meta, gates, counts = router(x, g, wr)
    cnt = counts[0, :N_EXPERTS].astype(I32)

    tiles_e = (cnt + tmf - 1) // tmf
    ends = jnp.cumsum(tiles_e)
    base = (ends - tiles_e) * tmf
    n_max = (2 * t) // tmf + N_EXPERTS
    slots_flat = (base[meta[:, 2:4]] + meta[:, 0:2]).reshape(-1)
    tile_e = jnp.minimum(jnp.sum((jnp.arange(n_max, dtype=I32)[:, None] >= ends[None, :]).astype(I32), axis=1),
                         N_EXPERTS - 1)
    n_act = ends[-1:].astype(I32)
    fill = jnp.concatenate([base + cnt, ends * tmf, n_act, jnp.full((1,), n_max, I32)]).astype(I32)

    xs = dispatch(x, slots_flat, fill, n_max * tmf, tmf)
    ys = moe_ffn(xs, g, wg, wu, wd, tile_e, n_act, tmf)
    return combine(x, slots_flat, gates, ys, g_final, final_norm)


def kernel(x, attn_norm, w_qkv, w_fgate, b_fgate, w_attn_out, conv_norm, w_pw1, b_pw1, w_dw, b_dw,
           conv_ln_g, conv_ln_b, w_pw2, b_pw2, dense_norm, w_dense_gate, w_dense_up, w_dense_down,
           moe_norm, w_router, w_exp_gate, w_exp_up, w_exp_down, final_norm):
    batch, seq, d = x.shape
    depth = attn_norm.shape[0] + conv_norm.shape[0]
    x = x.reshape(batch * seq, d)
    for layer in range(depth):
        j = layer // 2
        if layer % 2 == 0:
            qa, ka, v = attn_pre(x, attn_norm[j], w_qkv[j], w_fgate[j], b_fgate[j], seq)
            o = flash(qa, ka, v, batch, seq)
            x = ffn_dense(x, o, w_attn_out[j], dense_norm[j], w_dense_gate[j], w_dense_up[j], w_dense_down[j])
        else:
            u = conv_pre(x, conv_norm[j], w_pw1[j], b_pw1[j])
            x = conv_post(x, u, w_dw[j], b_dw[j], conv_ln_g[j], conv_ln_b[j], w_pw2[j], b_pw2[j], seq)
            x = moe_layer(x, moe_norm[j], w_router[j], w_exp_gate[j], w_exp_up[j], w_exp_down[j],
                          final_norm, final_norm=(layer == depth - 1))
    return x.reshape(batch, seq, d)
```

```python
import functools

import numpy as np
import jax
import jax.numpy as jnp
from jax import lax
from jax.experimental import pallas as pl
from jax.experimental.pallas import tpu as pltpu

N_HEADS = 16
HEAD_DIM = 64
CONV_WIDTH = 31
N_EXPERTS = 8
EPS = 1e-6

LANES = 128
HEAD_LANES = 128
FLASH_BLOCK = 256
V_ROWS = 80
LOG2E = 1.4426950408889634
CONV_HALO = 32
VMEM_LIMIT = 56 * 1024 * 1024
MASK_VALUE = -1e30

F32 = jnp.float32
BF16 = jnp.bfloat16
I32 = jnp.int32


def _params(*sem, flags=None):
    return pltpu.CompilerParams(dimension_semantics=sem, vmem_limit_bytes=VMEM_LIMIT, flags=flags)


def _rms(x, g):
    return x * lax.rsqrt(jnp.mean(x * x, axis=-1, keepdims=True) + EPS) * g


def _sigmoid(x):
    return 1.0 / (1.0 + jnp.exp(-x))


def _dot(a, b):
    return jnp.dot(a, b, preferred_element_type=F32)


def _split3(x):
    hi = x.astype(BF16)
    r1 = x - hi.astype(F32)
    mid = r1.astype(BF16)
    lo = (r1 - mid.astype(F32)).astype(BF16)
    return hi, mid, lo


def _decay_placement():
    eq = np.zeros((LANES, N_HEADS * HEAD_LANES), np.float32)
    ek = np.zeros((LANES, N_HEADS * HEAD_LANES), np.float32)
    for h in range(N_HEADS):
        base = h * HEAD_LANES + HEAD_DIM
        for r in range(3):
            eq[r * N_HEADS + h, base + r] = 1.0
            eq[3 * N_HEADS, base + 3 + r] = 1.0
            ek[3 * N_HEADS, base + r] = 1.0
            ek[r * N_HEADS + h, base + 3 + r] = -1.0
    return jnp.asarray(eq, BF16), jnp.asarray(ek, BF16)


def _attn_pre_kernel(x_ref, g_ref, wqkv_ref, wf_ref, bf_ref, eq_ref, ek_ref,
                     qa_ref, ka_ref, vt_ref, carry_ref, *, tiles_per_seq):
    i = pl.program_id(0)
    tm, d = x_ref.shape

    @pl.when(i % tiles_per_seq == 0)
    def _():
        carry_ref[...] = jnp.zeros_like(carry_ref)

    hb = _rms(x_ref[...], g_ref[...]).astype(BF16)

    z = _dot(hb, wf_ref[...]) + bf_ref[...]
    lf = jnp.minimum(z, 0.0) - jnp.log(1.0 + jnp.exp(-jnp.abs(z)))
    lane = lax.broadcasted_iota(I32, (tm, LANES), 1)
    lf = jnp.where(lane < N_HEADS, lf, 0.0)
    row = lax.broadcasted_iota(I32, (tm, tm), 0)
    col = lax.broadcasted_iota(I32, (tm, tm), 1)
    tri = jnp.where(col <= row, 1.0, 0.0).astype(BF16)
    hi, mid, lo = _split3(lf)
    c = _dot(tri, hi) + _dot(tri, mid) + _dot(tri, lo) + carry_ref[...]
    carry_ref[...] = c[tm - 1:tm, :]

    chi, cmid, clo = _split3(c * LOG2E)
    packed = (chi.astype(F32) + pltpu.roll(cmid.astype(F32), N_HEADS, 1)
              + pltpu.roll(clo.astype(F32), 2 * N_HEADS, 1)
              + jnp.where(lane == 3 * N_HEADS, 1.0, 0.0)).astype(BF16)
    augq = _dot(packed, eq_ref[...])
    augk = _dot(packed, ek_ref[...])

    low = lane < HEAD_DIM
    scale = HEAD_DIM ** -0.5 * LOG2E

    def emit(dst_ref, chunk, aug, head0, mul):
        for pp in range(chunk.shape[1] // LANES):
            pair = chunk[:, pp * LANES:(pp + 1) * LANES]
            if mul != 1.0:
                pair = pair * mul
            h0 = head0 + 2 * pp
            first = jnp.where(low, pair, aug[:, h0 * HEAD_LANES:(h0 + 1) * HEAD_LANES])
            second = jnp.where(low, pltpu.roll(pair, HEAD_DIM, 1),
                               aug[:, (h0 + 1) * HEAD_LANES:(h0 + 2) * HEAD_LANES])
            dst_ref[:, h0 * HEAD_LANES:(h0 + 1) * HEAD_LANES] = first.astype(BF16)
            dst_ref[:, (h0 + 1) * HEAD_LANES:(h0 + 2) * HEAD_LANES] = second.astype(BF16)

    cw = 256
    tk = vt_ref.shape[-1]
    for cc in range(d // cw):
        heads0 = cc * (cw // HEAD_DIM)
        emit(qa_ref, _dot(hb, wqkv_ref[:, cc * cw:(cc + 1) * cw]), augq, heads0, scale)
        emit(ka_ref, _dot(hb, wqkv_ref[:, d + cc * cw:d + (cc + 1) * cw]), augk, heads0, 1.0)
        v = _dot(hb, wqkv_ref[:, 2 * d + cc * cw:2 * d + (cc + 1) * cw])
        extra = jnp.where(lax.broadcasted_iota(I32, (V_ROWS - HEAD_DIM, tk), 0) == 0, 1.0, 0.0)
        for pp in range(cw // LANES):
            for jj in range(tm // tk):
                blk = v[jj * tk:(jj + 1) * tk, pp * LANES:(pp + 1) * LANES].T
                rows = jnp.concatenate([blk[:HEAD_DIM], extra, blk[HEAD_DIM:], extra], axis=0)
                vt_ref[cc * (cw // LANES) + pp, jj, :, :] = rows.astype(BF16)


def attn_pre(x, g, wqkv, wf, bf, batch, seq, tm=512):
    t, d = x.shape
    tm = min(tm, seq)
    tk = min(FLASH_BLOCK, seq)
    tiles_per_seq = seq // tm
    eq, ek = _decay_placement()
    wf_p = jnp.zeros((d, LANES), BF16).at[:, :N_HEADS].set(wf.astype(BF16))
    bf_p = jnp.zeros((1, LANES), F32).at[0, :N_HEADS].set(bf)
    na = N_HEADS * HEAD_LANES
    pairs = d // LANES
    full = lambda shape: pl.BlockSpec(shape, lambda i: (0, 0))
    return pl.pallas_call(
        functools.partial(_attn_pre_kernel, tiles_per_seq=tiles_per_seq),
        grid=(t // tm,),
        in_specs=[pl.BlockSpec((tm, d), lambda i: (i, 0)), full((1, d)), full((d, 3 * d)),
                  full((d, LANES)), full((1, LANES)), full((LANES, na)), full((LANES, na))],
        out_specs=[pl.BlockSpec((tm, na), lambda i: (i, 0)), pl.BlockSpec((tm, na), lambda i: (i, 0)),
                   pl.BlockSpec((None, pairs, tm // tk, 2 * V_ROWS, tk),
                                lambda i: (i // tiles_per_seq, 0, i % tiles_per_seq, 0, 0))],
        out_shape=[jax.ShapeDtypeStruct((t, na), BF16), jax.ShapeDtypeStruct((t, na), BF16),
                   jax.ShapeDtypeStruct((batch, pairs, seq // tk, 2 * V_ROWS, tk), BF16)],
        scratch_shapes=[pltpu.VMEM((1, LANES), F32)],
        compiler_params=_params("arbitrary"),
        name="attn_pre",
    )(x, g.reshape(1, d), wqkv.astype(BF16), wf_p, bf_p, eq, ek)


def _flash_kernel(qa_ref, ka_ref, vt_ref, o_ref, st_a, st_b, p_a, p_b, al_a, al_b, m_ref, acc_ref, *, tq, tk):
    qi = pl.program_id(2)

    def scores(n, st_ref):
        start = pl.multiple_of(n * tk, tk)
        for hh in range(2):
            lanes = slice(hh * HEAD_LANES, (hh + 1) * HEAD_LANES)
            st_ref[hh] = lax.dot_general(ka_ref[pl.ds(start, tk), lanes], qa_ref[:, lanes],
                                         (((1,), (1,)), ((), ())), preferred_element_type=F32)

    def softmax(n, st_ref, p_ref, al_ref, masked):
        for hh in range(2):
            st = st_ref[hh]
            if masked:
                key = n * tk + lax.broadcasted_iota(I32, (tk, tq), 0)
                qry = qi * tq + lax.broadcasted_iota(I32, (tk, tq), 1)
                st = jnp.where(key <= qry, st, MASK_VALUE)
            m_prev = m_ref[hh]
            m_new = jnp.maximum(m_prev, jnp.max(st, axis=0, keepdims=True))
            m_ref[hh] = m_new
            al_ref[hh] = jnp.exp2(m_prev - m_new)
            p_ref[hh] = jnp.exp2((st - m_new).astype(BF16))

    def values(n, p_ref, al_ref):
        for hh in range(2):
            vt = vt_ref[n, hh * V_ROWS:(hh + 1) * V_ROWS, :]
            acc_ref[hh] = al_ref[hh] * acc_ref[hh] + _dot(vt, p_ref[hh])

    m_ref[...] = jnp.full_like(m_ref, MASK_VALUE)
    acc_ref[...] = jnp.zeros_like(acc_ref)
    p_b[...] = jnp.zeros_like(p_b)
    al_b[...] = jnp.ones_like(al_b)
    scores(0, st_a)

    def body(t, carry):
        n0 = 2 * t
        scores(n0 + 1, st_b)
        values(jnp.maximum(n0 - 1, 0), p_b, al_b)
        softmax(n0, st_a, p_a, al_a, False)
        scores(n0 + 2, st_a)
        values(n0, p_a, al_a)
        softmax(n0 + 1, st_b, p_b, al_b, False)
        return carry

    lax.fori_loop(0, qi, body, 0)
    n0 = 2 * qi
    scores(n0 + 1, st_b)
    values(jnp.maximum(n0 - 1, 0), p_b, al_b)
    softmax(n0, st_a, p_a, al_a, True)
    values(n0, p_a, al_a)
    softmax(n0 + 1, st_b, p_b, al_b, True)
    values(n0 + 1, p_b, al_b)
    ot = jnp.concatenate([acc_ref[hh, :HEAD_DIM, :] / acc_ref[hh, HEAD_DIM:HEAD_DIM + 1, :] for hh in range(2)],
                         axis=0)
    o_ref[...] = ot.T.astype(BF16)


def flash(qa, ka, vt, batch, seq):
    t = qa.shape[0]
    tk = vt.shape[-1]
    tq = 2 * tk
    nq = seq // tq
    pairs = N_HEADS // 2
    return pl.pallas_call(
        functools.partial(_flash_kernel, tq=tq, tk=tk),
        grid=(batch, pairs, nq),
        in_specs=[pl.BlockSpec((tq, 2 * HEAD_LANES), lambda b, p, q: (b * nq + q, p)),
                  pl.BlockSpec((seq, 2 * HEAD_LANES), lambda b, p, q: (b, p)),
                  pl.BlockSpec((None, None, seq // tk, 2 * V_ROWS, tk), lambda b, p, q: (b, p, 0, 0, 0))],
        out_specs=pl.BlockSpec((tq, LANES), lambda b, p, q: (b * nq + q, p)),
        out_shape=jax.ShapeDtypeStruct((t, pairs * LANES), BF16),
        scratch_shapes=[pltpu.VMEM((2, tk, tq), F32), pltpu.VMEM((2, tk, tq), F32),
                        pltpu.VMEM((2, tk, tq), BF16), pltpu.VMEM((2, tk, tq), BF16),
                        pltpu.VMEM((2, 1, tq), F32), pltpu.VMEM((2, 1, tq), F32),
                        pltpu.VMEM((2, 1, tq), F32), pltpu.VMEM((2, V_ROWS, tq), F32)],
        compiler_params=_params("arbitrary", "arbitrary", "arbitrary"),
        name="flash",
    )(qa, ka, vt)


def _ffn_kernel(x_ref, o_ref, wo_ref, g_ref, wg_ref, wu_ref, wd_ref, out_ref, hb_ref):
    f = pl.program_id(1)

    @pl.when(f == 0)
    def _():
        x1 = x_ref[...] + _dot(o_ref[...], wo_ref[...])
        out_ref[...] = x1
        hb_ref[...] = _rms(x1, g_ref[...]).astype(BF16)

    hb = hb_ref[...]
    a = _dot(hb, wg_ref[...])
    u = _dot(hb, wu_ref[...])
    act = (a * _sigmoid(a) * u).astype(BF16)
    out_ref[...] += _dot(act, wd_ref[...])


def ffn_dense(x, o, wo, g, wg, wu, wd, tm=1024, tf=512):
    t, d = x.shape
    ff = wg.shape[1]
    tm = min(tm, t)
    tf = min(tf, ff)
    return pl.pallas_call(
        _ffn_kernel,
        grid=(t // tm, ff // tf),
        in_specs=[pl.BlockSpec((tm, d), lambda i, f: (i, 0)), pl.BlockSpec((tm, d), lambda i, f: (i, 0)),
                  pl.BlockSpec((d, d), lambda i, f: (0, 0)), pl.BlockSpec((1, d), lambda i, f: (0, 0)),
                  pl.BlockSpec((d, tf), lambda i, f: (0, f)), pl.BlockSpec((d, tf), lambda i, f: (0, f)),
                  pl.BlockSpec((tf, d), lambda i, f: (f, 0))],
        out_specs=pl.BlockSpec((tm, d), lambda i, f: (i, 0)),
        out_shape=jax.ShapeDtypeStruct((t, d), F32),
        scratch_shapes=[pltpu.VMEM((tm, d), BF16)],
        compiler_params=_params("arbitrary", "arbitrary"),
        name="ffn_dense",
    )(x, o, wo.astype(BF16), g.reshape(1, d), wg.astype(BF16), wu.astype(BF16), wd.astype(BF16))


def _conv_pre_kernel(x_ref, g_ref, w1_ref, b1_ref, u_ref):
    d = x_ref.shape[1]
    hb = _rms(x_ref[...], g_ref[...]).astype(BF16)
    cw = 256
    for cc in range(d // cw):
        lo, hi = cc * cw, (cc + 1) * cw
        a = _dot(hb, w1_ref[:, lo:hi]) + b1_ref[:, lo:hi]
        b = _dot(hb, w1_ref[:, d + lo:d + hi]) + b1_ref[:, d + lo:d + hi]
        u_ref[:, lo:hi] = a * _sigmoid(b)


def conv_pre(x, g, w1, b1, tm=512):
    t, d = x.shape
    tm = min(tm, t)
    return pl.pallas_call(
        _conv_pre_kernel,
        grid=(t // tm,),
        in_specs=[pl.BlockSpec((tm, d), lambda i: (i, 0)), pl.BlockSpec((1, d), lambda i: (0, 0)),
                  pl.BlockSpec((d, 2 * d), lambda i: (0, 0)), pl.BlockSpec((1, 2 * d), lambda i: (0, 0))],
        out_specs=pl.BlockSpec((tm, d), lambda i: (i, 0)),
        out_shape=jax.ShapeDtypeStruct((t, d), F32),
        compiler_params=_params("arbitrary"),
        name="conv_pre",
    )(x, g.reshape(1, d), w1.astype(BF16), b1.reshape(1, 2 * d))


def _conv_post_kernel(x_ref, u_ref, halo_ref, wdw_ref, bdw_ref, lng_ref, lnb_ref, w2_ref, b2_ref,
                      out_ref, ext_ref, y_ref, *, tiles_per_seq):
    i = pl.program_id(0)
    tm, d = x_ref.shape
    ext_ref[CONV_HALO:CONV_HALO + tm, :] = u_ref[...]

    @pl.when(i % tiles_per_seq == 0)
    def _():
        ext_ref[0:CONV_HALO, :] = jnp.zeros((CONV_HALO, d), F32)

    @pl.when(i % tiles_per_seq != 0)
    def _():
        ext_ref[0:CONV_HALO, :] = halo_ref[...]

    shift = CONV_HALO - (CONV_WIDTH - 1)
    rw, cw = min(64, tm), 256
    for cc in range(d // cw):
        cols = slice(cc * cw, (cc + 1) * cw)
        for rr in range(tm // rw):
            acc = jnp.zeros((rw, cw), F32)
            for j in range(CONV_WIDTH):
                r0 = rr * rw + shift + j
                acc = acc + wdw_ref[j:j + 1, cols] * ext_ref[r0:r0 + rw, cols]
            y_ref[rr * rw:(rr + 1) * rw, cols] = acc + bdw_ref[:, cols]

    y = y_ref[...]
    mu = jnp.mean(y, axis=-1, keepdims=True)
    yc = y - mu
    z = yc * lax.rsqrt(jnp.mean(yc * yc, axis=-1, keepdims=True) + EPS) * lng_ref[...] + lnb_ref[...]
    zb = (z * _sigmoid(z)).astype(BF16)
    out_ref[...] = x_ref[...] + _dot(zb, w2_ref[...]) + b2_ref[...]


def conv_post(x, u, wdw, bdw, lng, lnb, w2, b2, seq, tm=256):
    t, d = x.shape
    tm = min(tm, seq)
    hb = tm // CONV_HALO
    row = lambda a: a.reshape(1, d)
    return pl.pallas_call(
        functools.partial(_conv_post_kernel, tiles_per_seq=seq // tm),
        grid=(t // tm,),
        in_specs=[pl.BlockSpec((tm, d), lambda i: (i, 0)), pl.BlockSpec((tm, d), lambda i: (i, 0)),
                  pl.BlockSpec((CONV_HALO, d), lambda i: (jnp.maximum(i * hb - 1, 0), 0)),
                  pl.BlockSpec((CONV_WIDTH, d), lambda i: (0, 0)), pl.BlockSpec((1, d), lambda i: (0, 0)),
                  pl.BlockSpec((1, d), lambda i: (0, 0)), pl.BlockSpec((1, d), lambda i: (0, 0)),
                  pl.BlockSpec((d, d), lambda i: (0, 0)), pl.BlockSpec((1, d), lambda i: (0, 0))],
        out_specs=pl.BlockSpec((tm, d), lambda i: (i, 0)),
        out_shape=jax.ShapeDtypeStruct((t, d), F32),
        scratch_shapes=[pltpu.VMEM((tm + CONV_HALO, d), F32), pltpu.VMEM((tm, d), F32)],
        compiler_params=_params("arbitrary"),
        name="conv_post",
    )(x, u, u, wdw, row(bdw), row(lng), row(lnb), w2.astype(BF16), row(b2))


def _router_kernel(x_ref, g_ref, whi_ref, wlo_ref, slots_ref, gates_ref, counts_ref, carry_ref):
    i = pl.program_id(0)
    tm = x_ref.shape[0]

    @pl.when(i == 0)
    def _():
        carry_ref[...] = jnp.zeros_like(carry_ref)

    h = _rms(x_ref[...], g_ref[...])
    h_hi = h.astype(BF16)
    h_lo = (h - h_hi.astype(F32)).astype(BF16)
    logits = _dot(h_hi, whi_ref[...]) + _dot(h_lo, whi_ref[...]) + _dot(h_hi, wlo_ref[...])
    lane = lax.broadcasted_iota(I32, (tm, LANES), 1)
    lg = jnp.where(lane < N_EXPERTS, logits, -jnp.inf)
    m1 = jnp.max(lg, axis=1, keepdims=True)
    i1 = jnp.min(jnp.where(lg == m1, lane, LANES), axis=1, keepdims=True)
    lg2 = jnp.where(lane == i1, -jnp.inf, lg)
    m2 = jnp.max(lg2, axis=1, keepdims=True)
    i2 = jnp.min(jnp.where(lg2 == m2, lane, LANES), axis=1, keepdims=True)
    e2 = jnp.exp(m2 - m1)
    g0 = 1.0 / (1.0 + e2)
    g1 = e2 / (1.0 + e2)

    oh0 = lane == i1
    oh1 = lane == i2
    member = jnp.where(oh0 | oh1, 1.0, 0.0)
    row = lax.broadcasted_iota(I32, (tm, tm), 0)
    col = lax.broadcasted_iota(I32, (tm, tm), 1)
    before = jnp.where(col < row, 1.0, 0.0).astype(BF16)
    rank = _dot(before, member.astype(BF16)) + carry_ref[...]
    r0 = jnp.sum(jnp.where(oh0, rank, 0.0), axis=1, keepdims=True).astype(I32)
    r1 = jnp.sum(jnp.where(oh1, rank, 0.0), axis=1, keepdims=True).astype(I32)
    carry_ref[...] += jnp.sum(member, axis=0, keepdims=True)
    counts_ref[...] = carry_ref[...]
    slots_ref[...] = jnp.where(lane == 0, r0, jnp.where(lane == 1, r1, jnp.where(lane == 2, i1, jnp.where(lane == 3, i2, 0))))
    gates_ref[...] = jnp.where(lane == 0, g0, jnp.where(lane == 1, g1, 0.0))


def router(x, g, wr, tm=512):
    t, d = x.shape
    tm = min(tm, t)
    wr_p = jnp.zeros((d, LANES), F32).at[:, :N_EXPERTS].set(wr)
    w_hi = wr_p.astype(BF16)
    w_lo = (wr_p - w_hi.astype(F32)).astype(BF16)
    return pl.pallas_call(
        _router_kernel,
        grid=(t // tm,),
        in_specs=[pl.BlockSpec((tm, d), lambda i: (i, 0)), pl.BlockSpec((1, d), lambda i: (0, 0)),
                  pl.BlockSpec((d, LANES), lambda i: (0, 0)), pl.BlockSpec((d, LANES), lambda i: (0, 0))],
        out_specs=[pl.BlockSpec((tm, LANES), lambda i: (i, 0)), pl.BlockSpec((tm, LANES), lambda i: (i, 0)),
                   pl.BlockSpec((1, LANES), lambda i: (0, 0))],
        out_shape=[jax.ShapeDtypeStruct((t, LANES), I32), jax.ShapeDtypeStruct((t, LANES), F32),
                   jax.ShapeDtypeStruct((1, LANES), F32)],
        scratch_shapes=[pltpu.VMEM((1, LANES), F32)],
        compiler_params=_params("arbitrary"),
        name="router",
    )(x, g.reshape(1, d), w_hi, w_lo)


def _dispatch_kernel(fill_ref, slots_ref, x_ref, xs_ref, zero_ref, sem, *, tmf):
    i = pl.program_id(0)
    tm, d = x_ref.shape

    def body(r, carry):
        for k in range(2):
            s = slots_ref[0, 0, 2 * r + k]
            pltpu.make_async_copy(x_ref.at[pl.ds(r, 1), :], xs_ref.at[pl.ds(s, 1), :], sem).start()
        return carry

    lax.fori_loop(0, tm, body, 0)
    for k in range(2):
        pltpu.make_async_copy(x_ref, xs_ref.at[pl.ds(0, tm), :], sem).wait()

    @pl.when(i == pl.num_programs(0) - 1)
    def _():
        zero_ref[...] = jnp.zeros_like(zero_ref)
        sizes = []
        size = tmf // 2
        while size >= 8:
            sizes.append(size)
            size //= 2

        def zero_row(r, carry):
            row_copy = pltpu.make_async_copy(zero_ref.at[pl.ds(0, 1), :], xs_ref.at[pl.ds(r, 1), :], sem)
            row_copy.start()
            row_copy.wait()
            return carry

        def zero_tile(n, carry):
            tile_copy = pltpu.make_async_copy(zero_ref, xs_ref.at[pl.ds(pl.multiple_of(n * tmf, tmf), tmf), :], sem)
            tile_copy.start()
            tile_copy.wait()
            return carry

        for e in range(N_EXPERTS):
            end = fill_ref[e]
            end8 = (end + 7) // 8 * 8
            lax.fori_loop(end, end8, zero_row, 0)
            rem = fill_ref[N_EXPERTS + e] - end8
            pos = end8
            for size in sizes:
                chunk = pltpu.make_async_copy(zero_ref.at[pl.ds(0, size), :],
                                              xs_ref.at[pl.ds(pl.multiple_of(pos, 8), size), :], sem)
                take = (rem & size) != 0

                @pl.when(take)
                def _():
                    chunk.start()
                    chunk.wait()

                pos = pos + jnp.where(take, size, 0)
        lax.fori_loop(fill_ref[2 * N_EXPERTS], fill_ref[2 * N_EXPERTS + 1], zero_tile, 0)


def dispatch(x, slots_flat, fill, rows, tmf, tm=256):
    t, d = x.shape
    tm = min(tm, t)
    nt = t // tm
    return pl.pallas_call(
        functools.partial(_dispatch_kernel, tmf=tmf),
        grid_spec=pltpu.PrefetchScalarGridSpec(
            num_scalar_prefetch=1,
            grid=(nt,),
            in_specs=[pl.BlockSpec((1, 1, 2 * tm), lambda i, c: (i, 0, 0), memory_space=pltpu.SMEM),
                      pl.BlockSpec((tm, d), lambda i, c: (i, 0))],
            out_specs=pl.BlockSpec(memory_space=pl.ANY),
            scratch_shapes=[pltpu.VMEM((tmf, d), F32), pltpu.SemaphoreType.DMA(())],
        ),
        out_shape=jax.ShapeDtypeStruct((rows, d), F32),
        compiler_params=_params("arbitrary"),
        name="dispatch",
    )(fill, slots_flat.reshape(nt, 1, 2 * tm), x)


def _moe_ffn_kernel(te_ref, na_ref, xs_ref, g_ref, wg_ref, wu_ref, wd_ref, ys_ref, hb_ref):
    i = pl.program_id(0)
    f = pl.program_id(1)

    @pl.when(f == 0)
    def _():
        ys_ref[...] = jnp.zeros_like(ys_ref)

    @pl.when(i < na_ref[0])
    def _():
        @pl.when(f == 0)
        def _():
            hb_ref[...] = _rms(xs_ref[...], g_ref[...]).astype(BF16)

        hb = hb_ref[...]
        a = _dot(hb, wg_ref[...])
        u = _dot(hb, wu_ref[...])
        act = (a * _sigmoid(a) * u).astype(BF16)
        ys_ref[...] += _dot(act, wd_ref[...])


def moe_ffn(xs, g, wg, wu, wd, tile_e, n_act, tmf, tf=512):
    rows, d = xs.shape
    ff = wg.shape[2]
    tf = min(tf, ff)
    nf = ff // tf

    def live(i, na):
        return jnp.minimum(i, na[0] - 1)

    def fidx(i, f, na):
        return jnp.where(i < na[0], f, nf - 1)

    return pl.pallas_call(
        _moe_ffn_kernel,
        grid_spec=pltpu.PrefetchScalarGridSpec(
            num_scalar_prefetch=2,
            grid=(rows // tmf, nf),
            in_specs=[pl.BlockSpec((tmf, d), lambda i, f, te, na: (live(i, na), 0)),
                      pl.BlockSpec((1, d), lambda i, f, te, na: (0, 0)),
                      pl.BlockSpec((None, d, tf), lambda i, f, te, na: (te[live(i, na)], 0, fidx(i, f, na))),
                      pl.BlockSpec((None, d, tf), lambda i, f, te, na: (te[live(i, na)], 0, fidx(i, f, na))),
                      pl.BlockSpec((None, tf, d), lambda i, f, te, na: (te[live(i, na)], fidx(i, f, na), 0))],
            out_specs=pl.BlockSpec((tmf, d), lambda i, f, te, na: (i, 0)),
            scratch_shapes=[pltpu.VMEM((tmf, d), BF16)],
        ),
        out_shape=jax.ShapeDtypeStruct((rows, d), F32),
        compiler_params=_params("arbitrary", "arbitrary"),
        name="moe_ffn",
    )(tile_e, n_act, xs, g.reshape(1, d), wg.astype(BF16), wu.astype(BF16), wd.astype(BF16))


def _combine_kernel(slots_ref, x_ref, gates_ref, ys_ref, gf_ref, out_ref, ybuf_ref, sem, *, final_norm):
    tm, d = x_ref.shape

    def body(r, carry):
        for k in range(2):
            s = slots_ref[0, 0, 2 * r + k]
            pltpu.make_async_copy(ys_ref.at[pl.ds(s, 1), :], ybuf_ref.at[k, pl.ds(r, 1), :], sem).start()
        return carry

    lax.fori_loop(0, tm, body, 0)
    for k in range(2):
        pltpu.make_async_copy(ys_ref.at[pl.ds(0, tm), :], ybuf_ref.at[k], sem).wait()

    gates = gates_ref[...]
    out = x_ref[...] + gates[:, 0:1] * ybuf_ref[0] + gates[:, 1:2] * ybuf_ref[1]
    if final_norm:
        out = _rms(out, gf_ref[...])
    out_ref[...] = out


def combine(x, slots_flat, gates, ys, g_final, final_norm, tm=256):
    t, d = x.shape
    tm = min(tm, t)
    nt = t // tm
    return pl.pallas_call(
        functools.partial(_combine_kernel, final_norm=final_norm),
        grid=(nt,),
        in_specs=[pl.BlockSpec((1, 1, 2 * tm), lambda i: (i, 0, 0), memory_space=pltpu.SMEM),
                  pl.BlockSpec((tm, d), lambda i: (i, 0)), pl.BlockSpec((tm, LANES), lambda i: (i, 0)),
                  pl.BlockSpec(memory_space=pl.ANY), pl.BlockSpec((1, d), lambda i: (0, 0))],
        out_specs=pl.BlockSpec((tm, d), lambda i: (i, 0)),
        out_shape=jax.ShapeDtypeStruct((t, d), F32),
        scratch_shapes=[pltpu.VMEM((2, tm, d), F32), pltpu.SemaphoreType.DMA(())],
        compiler_params=_params("arbitrary"),
        name="combine",
    )(slots_flat.reshape(nt, 1, 2 * tm), x, gates, ys, g_final.reshape(1, d))


def moe_layer(x, g, wr, wg, wu, wd, g_final, final_norm, tmf=1024):
    t, d = x.shape
    tmf = min(tmf, t)
    meta, gates, counts = router(x, g, wr)
    cnt = counts[0, :N_EXPERTS].astype(I32)

    tiles_e = (cnt + tmf - 1) // tmf
    ends = jnp.cumsum(tiles_e)
    base = (ends - tiles_e) * tmf
    n_max = (2 * t) // tmf + N_EXPERTS
    slots_flat = (base[meta[:, 2:4]] + meta[:, 0:2]).reshape(-1)
    tile_e = jnp.minimum(jnp.sum((jnp.arange(n_max, dtype=I32)[:, None] >= ends[None, :]).astype(I32), axis=1),
                         N_EXPERTS - 1)
    n_act = ends[-1:].astype(I32)
    fill = jnp.concatenate([base + cnt, ends * tmf, n_act, jnp.full((1,), n_max, I32)]).astype(I32)

    xs = dispatch(x, slots_flat, fill, n_max * tmf, tmf)
    ys = moe_ffn(xs, g, wg, wu, wd, tile_e, n_act, tmf)
    return combine(x, slots_flat, gates, ys, g_final, final_norm)


def kernel(x, attn_norm, w_qkv, w_fgate, b_fgate, w_attn_out, conv_norm, w_pw1, b_pw1, w_dw, b_dw,
           conv_ln_g, conv_ln_b, w_pw2, b_pw2, dense_norm, w_dense_gate, w_dense_up, w_dense_down,
           moe_norm, w_router, w_exp_gate, w_exp_up, w_exp_down, final_norm):
    batch, seq, d = x.shape
    depth = attn_norm.shape[0] + conv_norm.shape[0]
    x = x.reshape(batch * seq, d)
    for layer in range(depth):
        j = layer // 2
        if layer % 2 == 0:
            qa, ka, vt = attn_pre(x, attn_norm[j], w_qkv[j], w_fgate[j], b_fgate[j], batch, seq)
            o = flash(qa, ka, vt, batch, seq)
            x = ffn_dense(x, o, w_attn_out[j], dense_norm[j], w_dense_gate[j], w_dense_up[j], w_dense_down[j])
        else:
            u = conv_pre(x, conv_norm[j], w_pw1[j], b_pw1[j])
            x = conv_post(x, u, w_dw[j], b_dw[j], conv_ln_g[j], conv_ln_b[j], w_pw2[j], b_pw2[j], seq)
            x = moe_layer(x, moe_norm[j], w_router[j], w_exp_gate[j], w_exp_up[j], w_exp_down[j],
                          final_norm, final_norm=(layer == depth - 1))
    return x.reshape(batch, seq, d)
```

```python
import functools

import numpy as np
import jax
import jax.numpy as jnp
from jax import lax
from jax.experimental import pallas as pl
from jax.experimental.pallas import tpu as pltpu

N_HEADS = 16
HEAD_DIM = 64
CONV_WIDTH = 31
N_EXPERTS = 8
EPS = 1e-6

LANES = 128
HEAD_LANES = 128
FLASH_BLOCK = 256
V_ROWS = 80
LOG2E = 1.4426950408889634
CONV_HALO = 32
CONV_TILE = 256
VMEM_LIMIT = 56 * 1024 * 1024
MASK_VALUE = -1e30

F32 = jnp.float32
BF16 = jnp.bfloat16
I32 = jnp.int32


def _params(*sem, flags=None):
    return pltpu.CompilerParams(dimension_semantics=sem, vmem_limit_bytes=VMEM_LIMIT, flags=flags)


def _rms(x, g):
    return x * lax.rsqrt(jnp.mean(x * x, axis=-1, keepdims=True) + EPS) * g


def _sigmoid(x):
    return 1.0 / (1.0 + jnp.exp(-x))


def _dot(a, b):
    return jnp.dot(a, b, preferred_element_type=F32)


def _split3(x):
    hi = x.astype(BF16)
    r1 = x - hi.astype(F32)
    mid = r1.astype(BF16)
    lo = (r1 - mid.astype(F32)).astype(BF16)
    return hi, mid, lo


def _decay_placement():
    eq = np.zeros((LANES, N_HEADS * HEAD_LANES), np.float32)
    ek = np.zeros((LANES, N_HEADS * HEAD_LANES), np.float32)
    for h in range(N_HEADS):
        base = h * HEAD_LANES + HEAD_DIM
        for r in range(3):
            eq[r * N_HEADS + h, base + r] = 1.0
            eq[3 * N_HEADS, base + 3 + r] = 1.0
            ek[3 * N_HEADS, base + r] = 1.0
            ek[r * N_HEADS + h, base + 3 + r] = -1.0
    return jnp.asarray(eq, BF16), jnp.asarray(ek, BF16)


def _attn_pre_kernel(x_ref, g_ref, wqkv_ref, wf_ref, bf_ref, eq_ref, ek_ref,
                     qa_ref, ka_ref, vt_ref, carry_ref, *, tiles_per_seq):
    i = pl.program_id(0)
    tm, d = x_ref.shape

    @pl.when(i % tiles_per_seq == 0)
    def _():
        carry_ref[...] = jnp.zeros_like(carry_ref)

    hb = _rms(x_ref[...], g_ref[...]).astype(BF16)

    z = _dot(hb, wf_ref[...]) + bf_ref[...]
    lf = jnp.minimum(z, 0.0) - jnp.log(1.0 + jnp.exp(-jnp.abs(z)))
    lane = lax.broadcasted_iota(I32, (tm, LANES), 1)
    lf = jnp.where(lane < N_HEADS, lf, 0.0)
    row = lax.broadcasted_iota(I32, (tm, tm), 0)
    col = lax.broadcasted_iota(I32, (tm, tm), 1)
    tri = jnp.where(col <= row, 1.0, 0.0).astype(BF16)
    hi, mid, lo = _split3(lf)
    c = _dot(tri, hi) + _dot(tri, mid) + _dot(tri, lo) + carry_ref[...]
    carry_ref[...] = c[tm - 1:tm, :]

    chi, cmid, clo = _split3(c * LOG2E)
    packed = (chi.astype(F32) + pltpu.roll(cmid.astype(F32), N_HEADS, 1)
              + pltpu.roll(clo.astype(F32), 2 * N_HEADS, 1)
              + jnp.where(lane == 3 * N_HEADS, 1.0, 0.0)).astype(BF16)
    augq = _dot(packed, eq_ref[...])
    augk = _dot(packed, ek_ref[...])

    low = lane < HEAD_DIM
    scale = HEAD_DIM ** -0.5 * LOG2E

    def emit(dst_ref, chunk, aug, head0, mul):
        for pp in range(chunk.shape[1] // LANES):
            pair = chunk[:, pp * LANES:(pp + 1) * LANES]
            if mul != 1.0:
                pair = pair * mul
            h0 = head0 + 2 * pp
            first = jnp.where(low, pair, aug[:, h0 * HEAD_LANES:(h0 + 1) * HEAD_LANES])
            second = jnp.where(low, pltpu.roll(pair, HEAD_DIM, 1),
                               aug[:, (h0 + 1) * HEAD_LANES:(h0 + 2) * HEAD_LANES])
            dst_ref[:, h0 * HEAD_LANES:(h0 + 1) * HEAD_LANES] = first.astype(BF16)
            dst_ref[:, (h0 + 1) * HEAD_LANES:(h0 + 2) * HEAD_LANES] = second.astype(BF16)

    cw = 256
    tk = vt_ref.shape[-1]
    for cc in range(d // cw):
        heads0 = cc * (cw // HEAD_DIM)
        emit(qa_ref, _dot(hb, wqkv_ref[:, cc * cw:(cc + 1) * cw]), augq, heads0, scale)
        emit(ka_ref, _dot(hb, wqkv_ref[:, d + cc * cw:d + (cc + 1) * cw]), augk, heads0, 1.0)
        v = _dot(hb, wqkv_ref[:, 2 * d + cc * cw:2 * d + (cc + 1) * cw])
        extra = jnp.where(lax.broadcasted_iota(I32, (V_ROWS - HEAD_DIM, tk), 0) == 0, 1.0, 0.0)
        for pp in range(cw // LANES):
            for jj in range(tm // tk):
                blk = v[jj * tk:(jj + 1) * tk, pp * LANES:(pp + 1) * LANES].T
                rows = jnp.concatenate([blk[:HEAD_DIM], extra, blk[HEAD_DIM:], extra], axis=0)
                vt_ref[cc * (cw // LANES) + pp, jj, :, :] = rows.astype(BF16)


def attn_pre(x, g, wqkv, wf, bf, batch, seq, tm=512):
    t, d = x.shape
    tm = min(tm, seq)
    tk = min(FLASH_BLOCK, seq // 4)
    tiles_per_seq = seq // tm
    eq, ek = _decay_placement()
    wf_p = jnp.zeros((d, LANES), BF16).at[:, :N_HEADS].set(wf.astype(BF16))
    bf_p = jnp.zeros((1, LANES), F32).at[0, :N_HEADS].set(bf)
    na = N_HEADS * HEAD_LANES
    pairs = d // LANES
    full = lambda shape: pl.BlockSpec(shape, lambda i: (0, 0))
    return pl.pallas_call(
        functools.partial(_attn_pre_kernel, tiles_per_seq=tiles_per_seq),
        grid=(t // tm,),
        in_specs=[pl.BlockSpec((tm, d), lambda i: (i, 0)), full((1, d)), full((d, 3 * d)),
                  full((d, LANES)), full((1, LANES)), full((LANES, na)), full((LANES, na))],
        out_specs=[pl.BlockSpec((tm, na), lambda i: (i, 0)), pl.BlockSpec((tm, na), lambda i: (i, 0)),
                   pl.BlockSpec((None, pairs, tm // tk, 2 * V_ROWS, tk),
                                lambda i: (i // tiles_per_seq, 0, i % tiles_per_seq, 0, 0))],
        out_shape=[jax.ShapeDtypeStruct((t, na), BF16), jax.ShapeDtypeStruct((t, na), BF16),
                   jax.ShapeDtypeStruct((batch, pairs, seq // tk, 2 * V_ROWS, tk), BF16)],
        scratch_shapes=[pltpu.VMEM((1, LANES), F32)],
        compiler_params=_params("arbitrary"),
        name="attn_pre",
    )(x, g.reshape(1, d), wqkv.astype(BF16), wf_p, bf_p, eq, ek)


def _flash_kernel(qa_ref, ka_ref, vt_ref, o_ref, *scratch, tq, tk):
    qi = pl.program_id(2)
    st_refs, p_refs, al_refs = scratch[0:4], scratch[4:8], scratch[8:12]
    m_ref, acc_ref = scratch[12:]

    def scores(n, slot, col0=0):
        start = pl.multiple_of(n * tk, tk)
        for hh in range(2):
            lanes = slice(hh * HEAD_LANES, (hh + 1) * HEAD_LANES)
            st_refs[slot][hh, :, col0:] = lax.dot_general(ka_ref[pl.ds(start, tk), lanes], qa_ref[col0:, lanes],
                                                          (((1,), (1,)), ((), ())), preferred_element_type=F32)

    def softmax(slot, diagonal, col0=0):
        for hh in range(2):
            st = st_refs[slot][hh, :, col0:]
            if diagonal:
                key = lax.broadcasted_iota(I32, st.shape, 0)
                qry = lax.broadcasted_iota(I32, st.shape, 1)
                st = jnp.where(key <= qry, st, MASK_VALUE)
            m_prev = m_ref[hh, :, col0:]
            m_new = jnp.maximum(m_prev, jnp.max(st, axis=0, keepdims=True))
            m_ref[hh, :, col0:] = m_new
            al_refs[slot][hh, :, col0:] = jnp.exp2(m_prev - m_new)
            p_refs[slot][hh, :, col0:] = jnp.exp2((st - m_new).astype(BF16))

    def values(n, slot, col0=0):
        for hh in range(2):
            vt = vt_ref[n, hh * V_ROWS:(hh + 1) * V_ROWS, :]
            acc_ref[hh, :, col0:] = (al_refs[slot][hh, :, col0:] * acc_ref[hh, :, col0:]
                                     + _dot(vt, p_refs[slot][hh, :, col0:]))

    def trip(n, diagonal):
        col = [slot * tk if diagonal else 0 for slot in range(4)]
        scores(n + 2, 2, col[2])
        scores(n + 3, 3, col[3])
        values(jnp.maximum(n - 2, 0), 2)
        values(jnp.maximum(n - 1, 0), 3)
        softmax(0, diagonal, col[0])
        softmax(1, diagonal, col[1])
        if not diagonal:
            scores(n + 4, 0)
            scores(n + 5, 1)
        values(n, 0, col[0])
        values(n + 1, 1, col[1])
        softmax(2, diagonal, col[2])
        softmax(3, diagonal, col[3])

    m_ref[...] = jnp.full_like(m_ref, MASK_VALUE)
    acc_ref[...] = jnp.zeros_like(acc_ref)
    for slot in (2, 3):
        p_refs[slot][...] = jnp.zeros_like(p_refs[slot])
        al_refs[slot][...] = jnp.ones_like(al_refs[slot])
    scores(0, 0)
    scores(1, 1)

    def body(t, carry):
        trip(4 * t, False)
        return carry

    lax.fori_loop(0, qi, body, 0)
    trip(4 * qi, True)
    values(4 * qi + 2, 2, 2 * tk)
    values(4 * qi + 3, 3, 3 * tk)
    ot = jnp.concatenate([acc_ref[hh, :HEAD_DIM, :] / acc_ref[hh, HEAD_DIM:HEAD_DIM + 1, :] for hh in range(2)],
                         axis=0)
    o_ref[...] = ot.T.astype(BF16)


def flash(qa, ka, vt, batch, seq):
    t = qa.shape[0]
    tk = vt.shape[-1]
    tq = 4 * tk
    nq = seq // tq
    pairs = N_HEADS // 2
    return pl.pallas_call(
        functools.partial(_flash_kernel, tq=tq, tk=tk),
        grid=(batch, pairs, nq),
        in_specs=[pl.BlockSpec((tq, 2 * HEAD_LANES), lambda b, p, q: (b * nq + q, p)),
                  pl.BlockSpec((seq, 2 * HEAD_LANES), lambda b, p, q: (b, p)),
                  pl.BlockSpec((None, None, seq // tk, 2 * V_ROWS, tk), lambda b, p, q: (b, p, 0, 0, 0))],
        out_specs=pl.BlockSpec((tq, LANES), lambda b, p, q: (b * nq + q, p)),
        out_shape=jax.ShapeDtypeStruct((t, pairs * LANES), BF16),
        scratch_shapes=([pltpu.VMEM((2, tk, tq), F32)] * 4 + [pltpu.VMEM((2, tk, tq), BF16)] * 4
                        + [pltpu.VMEM((2, 1, tq), F32)] * 4
                        + [pltpu.VMEM((2, 1, tq), F32), pltpu.VMEM((2, V_ROWS, tq), F32)]),
        compiler_params=_params("arbitrary", "arbitrary", "arbitrary"),
        name="flash",
    )(qa, ka, vt)


def _ffn_kernel(x_ref, o_ref, wo_ref, g_ref, wg_ref, wu_ref, wd_ref, out_ref, hb_ref):
    f = pl.program_id(1)

    @pl.when(f == 0)
    def _():
        x1 = x_ref[...] + _dot(o_ref[...], wo_ref[...])
        out_ref[...] = x1
        hb_ref[...] = _rms(x1, g_ref[...]).astype(BF16)

    hb = hb_ref[...]
    a = _dot(hb, wg_ref[...])
    u = _dot(hb, wu_ref[...])
    act = (a * _sigmoid(a) * u).astype(BF16)
    out_ref[...] += _dot(act, wd_ref[...])


def ffn_dense(x, o, wo, g, wg, wu, wd, tm=1024, tf=512):
    t, d = x.shape
    ff = wg.shape[1]
    tm = min(tm, t)
    tf = min(tf, ff)
    return pl.pallas_call(
        _ffn_kernel,
        grid=(t // tm, ff // tf),
        in_specs=[pl.BlockSpec((tm, d), lambda i, f: (i, 0)), pl.BlockSpec((tm, d), lambda i, f: (i, 0)),
                  pl.BlockSpec((d, d), lambda i, f: (0, 0)), pl.BlockSpec((1, d), lambda i, f: (0, 0)),
                  pl.BlockSpec((d, tf), lambda i, f: (0, f)), pl.BlockSpec((d, tf), lambda i, f: (0, f)),
                  pl.BlockSpec((tf, d), lambda i, f: (f, 0))],
        out_specs=pl.BlockSpec((tm, d), lambda i, f: (i, 0)),
        out_shape=jax.ShapeDtypeStruct((t, d), F32),
        scratch_shapes=[pltpu.VMEM((tm, d), BF16)],
        compiler_params=_params("arbitrary", "arbitrary"),
        name="ffn_dense",
    )(x, o, wo.astype(BF16), g.reshape(1, d), wg.astype(BF16), wu.astype(BF16), wd.astype(BF16))


def _conv_pre_kernel(x_ref, g_ref, w1_ref, b1_ref, u_ref):
    tm, d = x_ref.shape
    hb = _rms(x_ref[...], g_ref[...]).astype(BF16)
    cw = 256
    for cc in range(d // cw):
        lo, hi = cc * cw, (cc + 1) * cw
        a = _dot(hb, w1_ref[:, lo:hi]) + b1_ref[:, lo:hi]
        b = _dot(hb, w1_ref[:, d + lo:d + hi]) + b1_ref[:, d + lo:d + hi]
        u = a * _sigmoid(b)
        for kk in range(cw // LANES):
            k = cc * (cw // LANES) + kk
            u_ref[k * tm:(k + 1) * tm, :] = u[:, kk * LANES:(kk + 1) * LANES]


def conv_pre(x, g, w1, b1, tm):
    t, d = x.shape
    nl = d // LANES
    return pl.pallas_call(
        _conv_pre_kernel,
        grid=(t // tm,),
        in_specs=[pl.BlockSpec((tm, d), lambda i: (i, 0)), pl.BlockSpec((1, d), lambda i: (0, 0)),
                  pl.BlockSpec((d, 2 * d), lambda i: (0, 0)), pl.BlockSpec((1, 2 * d), lambda i: (0, 0))],
        out_specs=pl.BlockSpec((nl * tm, LANES), lambda i: (i, 0)),
        out_shape=jax.ShapeDtypeStruct((t * nl, LANES), F32),
        compiler_params=_params("arbitrary"),
        name="conv_pre",
    )(x, g.reshape(1, d), w1.astype(BF16), b1.reshape(1, 2 * d))


def _conv_post_kernel(x_ref, u_ref, wdw_ref, bdw_ref, lng_ref, lnb_ref, w2_ref, b2_ref,
                      out_ref, time_ref, ystack_ref, *, tiles_per_seq):
    i = pl.program_id(0)
    tm, d = x_ref.shape
    nl = d // LANES
    group = 16
    n_groups = tm // group

    @pl.when(i % tiles_per_seq == 0)
    def _():
        time_ref[0:CONV_HALO] = jnp.zeros((CONV_HALO, nl, LANES), F32)

    def to_time_major(r, carry):
        time_ref[CONV_HALO + r] = u_ref[pl.ds(r, nl, stride=tm), :]
        return carry

    lax.fori_loop(0, tm, to_time_major, 0, unroll=8)

    shift = CONV_HALO - (CONV_WIDTH - 1)

    def conv(g, carry):
        base = g * group
        accs = [bdw_ref[...]] * group
        for j in range(CONV_WIDTH):
            w = wdw_ref[j]
            accs = [accs[tt] + w * time_ref[base + shift + j + tt] for tt in range(group)]
        for tt in range(group):
            ystack_ref[pl.ds(base + tt, nl, stride=tm), :] = accs[tt]
        return carry

    lax.fori_loop(0, n_groups, conv, 0)
    time_ref[0:CONV_HALO] = time_ref[tm:tm + CONV_HALO]

    y = jnp.concatenate([ystack_ref[k * tm:(k + 1) * tm, :] for k in range(nl)], axis=1)
    mu = jnp.mean(y, axis=-1, keepdims=True)
    yc = y - mu
    z = yc * lax.rsqrt(jnp.mean(yc * yc, axis=-1, keepdims=True) + EPS) * lng_ref[...] + lnb_ref[...]
    zb = (z * _sigmoid(z)).astype(BF16)
    out_ref[...] = x_ref[...] + _dot(zb, w2_ref[...]) + b2_ref[...]


def conv_post(x, u, wdw, bdw, lng, lnb, w2, b2, seq, tm):
    t, d = x.shape
    nl = d // LANES
    row = lambda a: a.reshape(1, d)
    return pl.pallas_call(
        functools.partial(_conv_post_kernel, tiles_per_seq=seq // tm),
        grid=(t // tm,),
        in_specs=[pl.BlockSpec((tm, d), lambda i: (i, 0)), pl.BlockSpec((nl * tm, LANES), lambda i: (i, 0)),
                  pl.BlockSpec((CONV_WIDTH, nl, LANES), lambda i: (0, 0, 0)), pl.BlockSpec((nl, LANES), lambda i: (0, 0)),
                  pl.BlockSpec((1, d), lambda i: (0, 0)), pl.BlockSpec((1, d), lambda i: (0, 0)),
                  pl.BlockSpec((d, d), lambda i: (0, 0)), pl.BlockSpec((1, d), lambda i: (0, 0))],
        out_specs=pl.BlockSpec((tm, d), lambda i: (i, 0)),
        out_shape=jax.ShapeDtypeStruct((t, d), F32),
        scratch_shapes=[pltpu.VMEM((tm + CONV_HALO, nl, LANES), F32), pltpu.VMEM((nl * tm, LANES), F32)],
        compiler_params=_params("arbitrary"),
        name="conv_post",
    )(x, u, wdw.reshape(CONV_WIDTH, nl, LANES), bdw.reshape(nl, LANES), row(lng), row(lnb), w2.astype(BF16), row(b2))


def _router_kernel(x_ref, g_ref, whi_ref, wlo_ref, slots_ref, gates_ref, counts_ref, carry_ref):
    i = pl.program_id(0)
    tm = x_ref.shape[0]

    @pl.when(i == 0)
    def _():
        carry_ref[...] = jnp.zeros_like(carry_ref)

    h = _rms(x_ref[...], g_ref[...])
    h_hi = h.astype(BF16)
    h_lo = (h - h_hi.astype(F32)).astype(BF16)
    logits = _dot(h_hi, whi_ref[...]) + _dot(h_lo, whi_ref[...]) + _dot(h_hi, wlo_ref[...])
    lane = lax.broadcasted_iota(I32, (tm, LANES), 1)
    lg = jnp.where(lane < N_EXPERTS, logits, -jnp.inf)
    m1 = jnp.max(lg, axis=1, keepdims=True)
    i1 = jnp.min(jnp.where(lg == m1, lane, LANES), axis=1, keepdims=True)
    lg2 = jnp.where(lane == i1, -jnp.inf, lg)
    m2 = jnp.max(lg2, axis=1, keepdims=True)
    i2 = jnp.min(jnp.where(lg2 == m2, lane, LANES), axis=1, keepdims=True)
    e2 = jnp.exp(m2 - m1)
    g0 = 1.0 / (1.0 + e2)
    g1 = e2 / (1.0 + e2)

    oh0 = lane == i1
    oh1 = lane == i2
    member = jnp.where(oh0 | oh1, 1.0, 0.0)
    row = lax.broadcasted_iota(I32, (tm, tm), 0)
    col = lax.broadcasted_iota(I32, (tm, tm), 1)
    before = jnp.where(col < row, 1.0, 0.0).astype(BF16)
    rank = _dot(before, member.astype(BF16)) + carry_ref[...]
    r0 = jnp.sum(jnp.where(oh0, rank, 0.0), axis=1, keepdims=True).astype(I32)
    r1 = jnp.sum(jnp.where(oh1, rank, 0.0), axis=1, keepdims=True).astype(I32)
    carry_ref[...] += jnp.sum(member, axis=0, keepdims=True)
    counts_ref[...] = carry_ref[...]
    slots_ref[...] = jnp.where(lane == 0, r0, jnp.where(lane == 1, r1, jnp.where(lane == 2, i1, jnp.where(lane == 3, i2, 0))))
    gates_ref[...] = jnp.where(lane == 0, g0, jnp.where(lane == 1, g1, 0.0))


def router(x, g, wr, tm=512):
    t, d = x.shape
    tm = min(tm, t)
    wr_p = jnp.zeros((d, LANES), F32).at[:, :N_EXPERTS].set(wr)
    w_hi = wr_p.astype(BF16)
    w_lo = (wr_p - w_hi.astype(F32)).astype(BF16)
    return pl.pallas_call(
        _router_kernel,
        grid=(t // tm,),
        in_specs=[pl.BlockSpec((tm, d), lambda i: (i, 0)), pl.BlockSpec((1, d), lambda i: (0, 0)),
                  pl.BlockSpec((d, LANES), lambda i: (0, 0)), pl.BlockSpec((d, LANES), lambda i: (0, 0))],
        out_specs=[pl.BlockSpec((tm, LANES), lambda i: (i, 0)), pl.BlockSpec((tm, LANES), lambda i: (i, 0)),
                   pl.BlockSpec((1, LANES), lambda i: (0, 0))],
        out_shape=[jax.ShapeDtypeStruct((t, LANES), I32), jax.ShapeDtypeStruct((t, LANES), F32),
                   jax.ShapeDtypeStruct((1, LANES), F32)],
        scratch_shapes=[pltpu.VMEM((1, LANES), F32)],
        compiler_params=_params("arbitrary"),
        name="router",
    )(x, g.reshape(1, d), w_hi, w_lo)


def _dispatch_kernel(fill_ref, slots_ref, x_ref, xs_ref, zero_ref, sem, *, tmf):
    i = pl.program_id(0)
    tm, d = x_ref.shape

    def body(g, carry):
        for sub in range(8):
            r = pl.multiple_of(g * 8, 8) + sub
            for k in range(2):
                s = slots_ref[0, 0, 2 * r + k]
                pltpu.make_async_copy(x_ref.at[pl.ds(r, 1), :], xs_ref.at[pl.ds(s, 1), :], sem).start()
        return carry

    lax.fori_loop(0, tm // 8, body, 0)
    for k in range(2):
        pltpu.make_async_copy(x_ref, xs_ref.at[pl.ds(0, tm), :], sem).wait()

    @pl.when(i == pl.num_programs(0) - 1)
    def _():
        zero_ref[...] = jnp.zeros_like(zero_ref)
        sizes = []
        size = tmf // 2
        while size >= 8:
            sizes.append(size)
            size //= 2

        def zero_row(r, carry):
            row_copy = pltpu.make_async_copy(zero_ref.at[pl.ds(0, 1), :], xs_ref.at[pl.ds(r, 1), :], sem)
            row_copy.start()
            row_copy.wait()
            return carry

        def zero_tile(n, carry):
            tile_copy = pltpu.make_async_copy(zero_ref, xs_ref.at[pl.ds(pl.multiple_of(n * tmf, tmf), tmf), :], sem)
            tile_copy.start()
            tile_copy.wait()
            return carry

        for e in range(N_EXPERTS):
            end = fill_ref[e]
            end8 = (end + 7) // 8 * 8
            lax.fori_loop(end, end8, zero_row, 0)
            rem = fill_ref[N_EXPERTS + e] - end8
            pos = end8
            for size in sizes:
                chunk = pltpu.make_async_copy(zero_ref.at[pl.ds(0, size), :],
                                              xs_ref.at[pl.ds(pl.multiple_of(pos, 8), size), :], sem)
                take = (rem & size) != 0

                @pl.when(take)
                def _():
                    chunk.start()
                    chunk.wait()

                pos = pos + jnp.where(take, size, 0)
        lax.fori_loop(fill_ref[2 * N_EXPERTS], fill_ref[2 * N_EXPERTS + 1], zero_tile, 0)


def dispatch(x, slots_flat, fill, rows, tmf, tm=256):
    t, d = x.shape
    tm = min(tm, t)
    nt = t // tm
    return pl.pallas_call(
        functools.partial(_dispatch_kernel, tmf=tmf),
        grid_spec=pltpu.PrefetchScalarGridSpec(
            num_scalar_prefetch=1,
            grid=(nt,),
            in_specs=[pl.BlockSpec((1, 1, 2 * tm), lambda i, c: (i, 0, 0), memory_space=pltpu.SMEM),
                      pl.BlockSpec((tm, d), lambda i, c: (i, 0))],
            out_specs=pl.BlockSpec(memory_space=pl.ANY),
            scratch_shapes=[pltpu.VMEM((tmf, d), F32), pltpu.SemaphoreType.DMA(())],
        ),
        out_shape=jax.ShapeDtypeStruct((rows, d), F32),
        compiler_params=_params("arbitrary"),
        name="dispatch",
    )(fill, slots_flat.reshape(nt, 1, 2 * tm), x)


def _moe_ffn_kernel(te_ref, na_ref, xs_ref, g_ref, wg_ref, wu_ref, wd_ref, ys_ref, hb_ref):
    i = pl.program_id(0)
    f = pl.program_id(1)

    @pl.when(f == 0)
    def _():
        ys_ref[...] = jnp.zeros_like(ys_ref)

    @pl.when(i < na_ref[0])
    def _():
        @pl.when(f == 0)
        def _():
            hb_ref[...] = _rms(xs_ref[...], g_ref[...]).astype(BF16)

        hb = hb_ref[...]
        a = _dot(hb, wg_ref[...])
        u = _dot(hb, wu_ref[...])
        act = (a * _sigmoid(a) * u).astype(BF16)
        ys_ref[...] += _dot(act, wd_ref[...])


def moe_ffn(xs, g, wg, wu, wd, tile_e, n_act, tmf, tf=512):
    rows, d = xs.shape
    ff = wg.shape[2]
    tf = min(tf, ff)
    nf = ff // tf

    def live(i, na):
        return jnp.minimum(i, na[0] - 1)

    def fidx(i, f, na):
        return jnp.where(i < na[0], f, nf - 1)

    return pl.pallas_call(
        _moe_ffn_kernel,
        grid_spec=pltpu.PrefetchScalarGridSpec(
            num_scalar_prefetch=2,
            grid=(rows // tmf, nf),
            in_specs=[pl.BlockSpec((tmf, d), lambda i, f, te, na: (live(i, na), 0)),
                      pl.BlockSpec((1, d), lambda i, f, te, na: (0, 0)),
                      pl.BlockSpec((None, d, tf), lambda i, f, te, na: (te[live(i, na)], 0, fidx(i, f, na))),
                      pl.BlockSpec((None, d, tf), lambda i, f, te, na: (te[live(i, na)], 0, fidx(i, f, na))),
                      pl.BlockSpec((None, tf, d), lambda i, f, te, na: (te[live(i, na)], fidx(i, f, na), 0))],
            out_specs=pl.BlockSpec((tmf, d), lambda i, f, te, na: (i, 0)),
            scratch_shapes=[pltpu.VMEM((tmf, d), BF16)],
        ),
        out_shape=jax.ShapeDtypeStruct((rows, d), F32),
        compiler_params=_params("arbitrary", "arbitrary"),
        name="moe_ffn",
    )(tile_e, n_act, xs, g.reshape(1, d), wg.astype(BF16), wu.astype(BF16), wd.astype(BF16))


def _combine_kernel(slots_ref, x_ref, gates_ref, ys_ref, gf_ref, out_ref, ybuf_ref, sem, *, final_norm):
    tm, d = x_ref.shape

    def body(g, carry):
        for sub in range(8):
            r = pl.multiple_of(g * 8, 8) + sub
            for k in range(2):
                s = slots_ref[0, 0, 2 * r + k]
                pltpu.make_async_copy(ys_ref.at[pl.ds(s, 1), :], ybuf_ref.at[k, pl.ds(r, 1), :], sem).start()
        return carry

    lax.fori_loop(0, tm // 8, body, 0)
    for k in range(2):
        pltpu.make_async_copy(ys_ref.at[pl.ds(0, tm), :], ybuf_ref.at[k], sem).wait()

    gates = gates_ref[...]
    out = x_ref[...] + gates[:, 0:1] * ybuf_ref[0] + gates[:, 1:2] * ybuf_ref[1]
    if final_norm:
        out = _rms(out, gf_ref[...])
    out_ref[...] = out


def combine(x, slots_flat, gates, ys, g_final, final_norm, tm=256):
    t, d = x.shape
    tm = min(tm, t)
    nt = t // tm
    return pl.pallas_call(
        functools.partial(_combine_kernel, final_norm=final_norm),
        grid=(nt,),
        in_specs=[pl.BlockSpec((1, 1, 2 * tm), lambda i: (i, 0, 0), memory_space=pltpu.SMEM),
                  pl.BlockSpec((tm, d), lambda i: (i, 0)), pl.BlockSpec((tm, LANES), lambda i: (i, 0)),
                  pl.BlockSpec(memory_space=pl.ANY), pl.BlockSpec((1, d), lambda i: (0, 0))],
        out_specs=pl.BlockSpec((tm, d), lambda i: (i, 0)),
        out_shape=jax.ShapeDtypeStruct((t, d), F32),
        scratch_shapes=[pltpu.VMEM((2, tm, d), F32), pltpu.SemaphoreType.DMA(())],
        compiler_params=_params("arbitrary"),
        name="combine",
    )(slots_flat.reshape(nt, 1, 2 * tm), x, gates, ys, g_final.reshape(1, d))


def moe_layer(x, g, wr, wg, wu, wd, g_final, final_norm, tmf=1024):
    t, d = x.shape
    tmf = min(tmf, t)
    meta, gates, counts = router(x, g, wr)
    cnt = counts[0, :N_EXPERTS].astype(I32)

    tiles_e = (cnt + tmf - 1) // tmf
    ends = jnp.cumsum(tiles_e)
    base = (ends - tiles_e) * tmf
    n_max = (2 * t) // tmf + N_EXPERTS
    slots_flat = (base[meta[:, 2:4]] + meta[:, 0:2]).reshape(-1)
    tile_e = jnp.minimum(jnp.sum((jnp.arange(n_max, dtype=I32)[:, None] >= ends[None, :]).astype(I32), axis=1),
                         N_EXPERTS - 1)
    n_act = ends[-1:].astype(I32)
    fill = jnp.concatenate([base + cnt, ends * tmf, n_act, jnp.full((1,), n_max, I32)]).astype(I32)

    xs = dispatch(x, slots_flat, fill, n_max * tmf, tmf)
    ys = moe_ffn(xs, g, wg, wu, wd, tile_e, n_act, tmf)
    return combine(x, slots_flat, gates, ys, g_final, final_norm)


def kernel(x, attn_norm, w_qkv, w_fgate, b_fgate, w_attn_out, conv_norm, w_pw1, b_pw1, w_dw, b_dw,
           conv_ln_g, conv_ln_b, w_pw2, b_pw2, dense_norm, w_dense_gate, w_dense_up, w_dense_down,
           moe_norm, w_router, w_exp_gate, w_exp_up, w_exp_down, final_norm):
    batch, seq, d = x.shape
    depth = attn_norm.shape[0] + conv_norm.shape[0]
    x = x.reshape(batch * seq, d)
    for layer in range(depth):
        j = layer // 2
        if layer % 2 == 0:
            qa, ka, vt = attn_pre(x, attn_norm[j], w_qkv[j], w_fgate[j], b_fgate[j], batch, seq)
            o = flash(qa, ka, vt, batch, seq)
            x = ffn_dense(x, o, w_attn_out[j], dense_norm[j], w_dense_gate[j], w_dense_up[j], w_dense_down[j])
        else:
            u = conv_pre(x, conv_norm[j], w_pw1[j], b_pw1[j], CONV_TILE)
            x = conv_post(x, u, w_dw[j], b_dw[j], conv_ln_g[j], conv_ln_b[j], w_pw2[j], b_pw2[j], seq, CONV_TILE)
            x = moe_layer(x, moe_norm[j], w_router[j], w_exp_gate[j], w_exp_up[j], w_exp_down[j],
                          final_norm, final_norm=(layer == depth - 1))
    return x.reshape(batch, seq, d)
```

```python
import functools

import numpy as np
import jax
import jax.numpy as jnp
from jax import lax
from jax.experimental import pallas as pl
from jax.experimental.pallas import tpu as pltpu

N_HEADS = 16
HEAD_DIM = 64
CONV_WIDTH = 31
N_EXPERTS = 8
EPS = 1e-6

LANES = 128
HEAD_LANES = 128
FLASH_BLOCK = 256
V_ROWS = 80
LOG2E = 1.4426950408889634
CONV_HALO = 32
CONV_TILE = 512
VMEM_LIMIT = 56 * 1024 * 1024
MASK_VALUE = -1e30

F32 = jnp.float32
BF16 = jnp.bfloat16
I32 = jnp.int32


def _params(*sem, flags=None):
    return pltpu.CompilerParams(dimension_semantics=sem, vmem_limit_bytes=VMEM_LIMIT, flags=flags)


def _rms(x, g):
    return x * lax.rsqrt(jnp.mean(x * x, axis=-1, keepdims=True) + EPS) * g


def _sigmoid(x):
    return 1.0 / (1.0 + jnp.exp(-x))


def _dot(a, b):
    return jnp.dot(a, b, preferred_element_type=F32)


def _split3(x):
    hi = x.astype(BF16)
    r1 = x - hi.astype(F32)
    mid = r1.astype(BF16)
    lo = (r1 - mid.astype(F32)).astype(BF16)
    return hi, mid, lo


def _decay_placement():
    eq = np.zeros((LANES, N_HEADS * HEAD_LANES), np.float32)
    ek = np.zeros((LANES, N_HEADS * HEAD_LANES), np.float32)
    for h in range(N_HEADS):
        base = h * HEAD_LANES + HEAD_DIM
        for r in range(3):
            eq[r * N_HEADS + h, base + r] = 1.0
            eq[3 * N_HEADS, base + 3 + r] = 1.0
            ek[3 * N_HEADS, base + r] = 1.0
            ek[r * N_HEADS + h, base + 3 + r] = -1.0
    return jnp.asarray(eq, BF16), jnp.asarray(ek, BF16)


def _attn_pre_kernel(x_ref, g_ref, wqkv_ref, wf_ref, bf_ref, eq_ref, ek_ref,
                     qa_ref, ka_ref, vt_ref, carry_ref, *, tiles_per_seq):
    i = pl.program_id(0)
    tm, d = x_ref.shape

    @pl.when(i % tiles_per_seq == 0)
    def _():
        carry_ref[...] = jnp.zeros_like(carry_ref)

    hb = _rms(x_ref[...], g_ref[...]).astype(BF16)

    z = _dot(hb, wf_ref[...]) + bf_ref[...]
    lf = jnp.minimum(z, 0.0) - jnp.log(1.0 + jnp.exp(-jnp.abs(z)))
    lane = lax.broadcasted_iota(I32, (tm, LANES), 1)
    lf = jnp.where(lane < N_HEADS, lf, 0.0)
    row = lax.broadcasted_iota(I32, (tm, tm), 0)
    col = lax.broadcasted_iota(I32, (tm, tm), 1)
    tri = jnp.where(col <= row, 1.0, 0.0).astype(BF16)
    hi, mid, lo = _split3(lf)
    c = _dot(tri, hi) + _dot(tri, mid) + _dot(tri, lo) + carry_ref[...]
    carry_ref[...] = c[tm - 1:tm, :]

    chi, cmid, clo = _split3(c * LOG2E)
    packed = (chi.astype(F32) + pltpu.roll(cmid.astype(F32), N_HEADS, 1)
              + pltpu.roll(clo.astype(F32), 2 * N_HEADS, 1)
              + jnp.where(lane == 3 * N_HEADS, 1.0, 0.0)).astype(BF16)
    augq = _dot(packed, eq_ref[...])
    augk = _dot(packed, ek_ref[...])

    low = lane < HEAD_DIM
    scale = HEAD_DIM ** -0.5 * LOG2E

    def emit(dst_ref, chunk, aug, head0, mul):
        for pp in range(chunk.shape[1] // LANES):
            pair = chunk[:, pp * LANES:(pp + 1) * LANES]
            if mul != 1.0:
                pair = pair * mul
            h0 = head0 + 2 * pp
            first = jnp.where(low, pair, aug[:, h0 * HEAD_LANES:(h0 + 1) * HEAD_LANES])
            second = jnp.where(low, pltpu.roll(pair, HEAD_DIM, 1),
                               aug[:, (h0 + 1) * HEAD_LANES:(h0 + 2) * HEAD_LANES])
            dst_ref[:, h0 * HEAD_LANES:(h0 + 1) * HEAD_LANES] = first.astype(BF16)
            dst_ref[:, (h0 + 1) * HEAD_LANES:(h0 + 2) * HEAD_LANES] = second.astype(BF16)

    cw = 256
    tk = vt_ref.shape[-1]
    for cc in range(d // cw):
        heads0 = cc * (cw // HEAD_DIM)
        emit(qa_ref, _dot(hb, wqkv_ref[:, cc * cw:(cc + 1) * cw]), augq, heads0, scale)
        emit(ka_ref, _dot(hb, wqkv_ref[:, d + cc * cw:d + (cc + 1) * cw]), augk, heads0, 1.0)
        v = _dot(hb, wqkv_ref[:, 2 * d + cc * cw:2 * d + (cc + 1) * cw])
        extra = jnp.where(lax.broadcasted_iota(I32, (V_ROWS - HEAD_DIM, tk), 0) == 0, 1.0, 0.0)
        for pp in range(cw // LANES):
            for jj in range(tm // tk):
                blk = v[jj * tk:(jj + 1) * tk, pp * LANES:(pp + 1) * LANES].T
                rows = jnp.concatenate([blk[:HEAD_DIM], extra, blk[HEAD_DIM:], extra], axis=0)
                vt_ref[cc * (cw // LANES) + pp, jj, :, :] = rows.astype(BF16)


def attn_pre(x, g, wqkv, wf, bf, batch, seq, tm=512):
    t, d = x.shape
    tm = min(tm, seq)
    tk = min(FLASH_BLOCK, seq // 4)
    tiles_per_seq = seq // tm
    eq, ek = _decay_placement()
    wf_p = jnp.zeros((d, LANES), BF16).at[:, :N_HEADS].set(wf.astype(BF16))
    bf_p = jnp.zeros((1, LANES), F32).at[0, :N_HEADS].set(bf)
    na = N_HEADS * HEAD_LANES
    pairs = d // LANES
    full = lambda shape: pl.BlockSpec(shape, lambda i: (0, 0))
    return pl.pallas_call(
        functools.partial(_attn_pre_kernel, tiles_per_seq=tiles_per_seq),
        grid=(t // tm,),
        in_specs=[pl.BlockSpec((tm, d), lambda i: (i, 0)), full((1, d)), full((d, 3 * d)),
                  full((d, LANES)), full((1, LANES)), full((LANES, na)), full((LANES, na))],
        out_specs=[pl.BlockSpec((tm, na), lambda i: (i, 0)), pl.BlockSpec((tm, na), lambda i: (i, 0)),
                   pl.BlockSpec((None, pairs, tm // tk, 2 * V_ROWS, tk),
                                lambda i: (i // tiles_per_seq, 0, i % tiles_per_seq, 0, 0))],
        out_shape=[jax.ShapeDtypeStruct((t, na), BF16), jax.ShapeDtypeStruct((t, na), BF16),
                   jax.ShapeDtypeStruct((batch, pairs, seq // tk, 2 * V_ROWS, tk), BF16)],
        scratch_shapes=[pltpu.VMEM((1, LANES), F32)],
        compiler_params=_params("arbitrary"),
        name="attn_pre",
    )(x, g.reshape(1, d), wqkv.astype(BF16), wf_p, bf_p, eq, ek)


def _flash_kernel(qa_ref, ka_ref, vt_ref, o_ref, *scratch, tq, tk):
    qi = pl.program_id(2)
    st_refs, p_refs, al_refs = scratch[0:4], scratch[4:8], scratch[8:12]
    m_ref, acc_ref = scratch[12:]

    def scores(n, slot, col0=0):
        start = pl.multiple_of(n * tk, tk)
        for hh in range(2):
            lanes = slice(hh * HEAD_LANES, (hh + 1) * HEAD_LANES)
            st_refs[slot][hh, :, col0:] = lax.dot_general(ka_ref[pl.ds(start, tk), lanes], qa_ref[col0:, lanes],
                                                          (((1,), (1,)), ((), ())), preferred_element_type=F32)

    def softmax(slot, diagonal, col0=0):
        def load(hh):
            st = st_refs[slot][hh, :, col0:]
            if diagonal:
                key = lax.broadcasted_iota(I32, st.shape, 0)
                qry = lax.broadcasted_iota(I32, st.shape, 1)
                st = jnp.where(key <= qry, st, MASK_VALUE)
            return st

        for hh in range(2):
            m_prev = m_ref[hh, :, col0:]
            m_new = jnp.maximum(m_prev, jnp.max(load(hh), axis=0, keepdims=True))
            m_ref[hh, :, col0:] = m_new
            al_refs[slot][hh, :, col0:] = jnp.exp2(m_prev - m_new)
            p_refs[slot][hh, :, col0:] = jnp.exp2((load(hh) - m_new).astype(BF16))

    def values(n, slot, col0=0):
        for hh in range(2):
            vt = vt_ref[n, hh * V_ROWS:(hh + 1) * V_ROWS, :]
            acc_ref[hh, :, col0:] = (al_refs[slot][hh, :, col0:] * acc_ref[hh, :, col0:]
                                     + _dot(vt, p_refs[slot][hh, :, col0:]))

    def trip(n, diagonal):
        col = [slot * tk if diagonal else 0 for slot in range(4)]
        scores(n + 2, 2, col[2])
        scores(n + 3, 3, col[3])
        values(jnp.maximum(n - 2, 0), 2)
        values(jnp.maximum(n - 1, 0), 3)
        softmax(0, diagonal, col[0])
        softmax(1, diagonal, col[1])
        if not diagonal:
            scores(n + 4, 0)
            scores(n + 5, 1)
        values(n, 0, col[0])
        values(n + 1, 1, col[1])
        softmax(2, diagonal, col[2])
        softmax(3, diagonal, col[3])

    m_ref[...] = jnp.full_like(m_ref, MASK_VALUE)
    acc_ref[...] = jnp.zeros_like(acc_ref)
    for slot in (2, 3):
        p_refs[slot][...] = jnp.zeros_like(p_refs[slot])
        al_refs[slot][...] = jnp.ones_like(al_refs[slot])
    scores(0, 0)
    scores(1, 1)

    def body(t, carry):
        trip(4 * t, False)
        return carry

    lax.fori_loop(0, qi, body, 0)
    trip(4 * qi, True)
    values(4 * qi + 2, 2, 2 * tk)
    values(4 * qi + 3, 3, 3 * tk)
    ot = jnp.concatenate([acc_ref[hh, :HEAD_DIM, :] / acc_ref[hh, HEAD_DIM:HEAD_DIM + 1, :] for hh in range(2)],
                         axis=0)
    o_ref[...] = ot.T.astype(BF16)


def flash(qa, ka, vt, batch, seq):
    t = qa.shape[0]
    tk = vt.shape[-1]
    tq = 4 * tk
    nq = seq // tq
    pairs = N_HEADS // 2
    return pl.pallas_call(
        functools.partial(_flash_kernel, tq=tq, tk=tk),
        grid=(batch, pairs, nq),
        in_specs=[pl.BlockSpec((tq, 2 * HEAD_LANES), lambda b, p, q: (b * nq + q, p)),
                  pl.BlockSpec((seq, 2 * HEAD_LANES), lambda b, p, q: (b, p)),
                  pl.BlockSpec((None, None, seq // tk, 2 * V_ROWS, tk), lambda b, p, q: (b, p, 0, 0, 0))],
        out_specs=pl.BlockSpec((tq, LANES), lambda b, p, q: (b * nq + q, p)),
        out_shape=jax.ShapeDtypeStruct((t, pairs * LANES), BF16),
        scratch_shapes=([pltpu.VMEM((2, tk, tq), F32)] * 4 + [pltpu.VMEM((2, tk, tq), BF16)] * 4
                        + [pltpu.VMEM((2, 1, tq), F32)] * 4
                        + [pltpu.VMEM((2, 1, tq), F32), pltpu.VMEM((2, V_ROWS, tq), F32)]),
        compiler_params=_params("arbitrary", "arbitrary", "arbitrary"),
        name="flash",
    )(qa, ka, vt)


def _ffn_kernel(x_ref, o_ref, wo_ref, g_ref, wg_ref, wu_ref, wd_ref, out_ref, hb_ref):
    f = pl.program_id(1)

    @pl.when(f == 0)
    def _():
        x1 = x_ref[...] + _dot(o_ref[...], wo_ref[...])
        out_ref[...] = x1
        hb_ref[...] = _rms(x1, g_ref[...]).astype(BF16)

    hb = hb_ref[...]
    a = _dot(hb, wg_ref[...])
    u = _dot(hb, wu_ref[...])
    act = (a * _sigmoid(a) * u).astype(BF16)
    out_ref[...] += _dot(act, wd_ref[...])


def ffn_dense(x, o, wo, g, wg, wu, wd, tm=1024, tf=512):
    t, d = x.shape
    ff = wg.shape[1]
    tm = min(tm, t)
    tf = min(tf, ff)
    return pl.pallas_call(
        _ffn_kernel,
        grid=(t // tm, ff // tf),
        in_specs=[pl.BlockSpec((tm, d), lambda i, f: (i, 0)), pl.BlockSpec((tm, d), lambda i, f: (i, 0)),
                  pl.BlockSpec((d, d), lambda i, f: (0, 0)), pl.BlockSpec((1, d), lambda i, f: (0, 0)),
                  pl.BlockSpec((d, tf), lambda i, f: (0, f)), pl.BlockSpec((d, tf), lambda i, f: (0, f)),
                  pl.BlockSpec((tf, d), lambda i, f: (f, 0))],
        out_specs=pl.BlockSpec((tm, d), lambda i, f: (i, 0)),
        out_shape=jax.ShapeDtypeStruct((t, d), F32),
        scratch_shapes=[pltpu.VMEM((tm, d), BF16)],
        compiler_params=_params("arbitrary", "arbitrary"),
        name="ffn_dense",
    )(x, o, wo.astype(BF16), g.reshape(1, d), wg.astype(BF16), wu.astype(BF16), wd.astype(BF16))


def _conv_pre_kernel(x_ref, g_ref, w1_ref, b1_ref, u_ref):
    tm, d = x_ref.shape
    hb = _rms(x_ref[...], g_ref[...]).astype(BF16)
    cw = 256
    for cc in range(d // cw):
        lo, hi = cc * cw, (cc + 1) * cw
        a = _dot(hb, w1_ref[:, lo:hi]) + b1_ref[:, lo:hi]
        b = _dot(hb, w1_ref[:, d + lo:d + hi]) + b1_ref[:, d + lo:d + hi]
        u = a * _sigmoid(b)
        for kk in range(cw // LANES):
            k = cc * (cw // LANES) + kk
            u_ref[k * tm:(k + 1) * tm, :] = u[:, kk * LANES:(kk + 1) * LANES]


def conv_pre(x, g, w1, b1, tm):
    t, d = x.shape
    nl = d // LANES
    return pl.pallas_call(
        _conv_pre_kernel,
        grid=(t // tm,),
        in_specs=[pl.BlockSpec((tm, d), lambda i: (i, 0)), pl.BlockSpec((1, d), lambda i: (0, 0)),
                  pl.BlockSpec((d, 2 * d), lambda i: (0, 0)), pl.BlockSpec((1, 2 * d), lambda i: (0, 0))],
        out_specs=pl.BlockSpec((nl * tm, LANES), lambda i: (i, 0)),
        out_shape=jax.ShapeDtypeStruct((t * nl, LANES), F32),
        compiler_params=_params("arbitrary"),
        name="conv_pre",
    )(x, g.reshape(1, d), w1.astype(BF16), b1.reshape(1, 2 * d))


def _conv_post_kernel(x_ref, u_ref, wdw_ref, bdw_ref, lng_ref, lnb_ref, w2_ref, b2_ref,
                      out_ref, time_ref, ystack_ref, *, tiles_per_seq):
    i = pl.program_id(0)
    tm, d = x_ref.shape
    nl = d // LANES
    group = 16
    n_groups = tm // group

    @pl.when(i % tiles_per_seq == 0)
    def _():
        time_ref[0:CONV_HALO] = jnp.zeros((CONV_HALO, nl, LANES), F32)

    def to_time_major(r, carry):
        time_ref[CONV_HALO + r] = u_ref[pl.ds(r, nl, stride=tm), :]
        return carry

    lax.fori_loop(0, tm, to_time_major, 0, unroll=8)

    shift = CONV_HALO - (CONV_WIDTH - 1)

    def conv(g, carry):
        base = g * group
        accs = [bdw_ref[...]] * group
        for j in range(CONV_WIDTH):
            w = wdw_ref[j]
            accs = [accs[tt] + w * time_ref[base + shift + j + tt] for tt in range(group)]
        for tt in range(group):
            ystack_ref[pl.ds(base + tt, nl, stride=tm), :] = accs[tt]
        return carry

    lax.fori_loop(0, n_groups, conv, 0)
    time_ref[0:CONV_HALO] = time_ref[tm:tm + CONV_HALO]

    y = jnp.concatenate([ystack_ref[k * tm:(k + 1) * tm, :] for k in range(nl)], axis=1)
    mu = jnp.mean(y, axis=-1, keepdims=True)
    yc = y - mu
    z = yc * lax.rsqrt(jnp.mean(yc * yc, axis=-1, keepdims=True) + EPS) * lng_ref[...] + lnb_ref[...]
    zb = (z * _sigmoid(z)).astype(BF16)
    out_ref[...] = x_ref[...] + _dot(zb, w2_ref[...]) + b2_ref[...]


def conv_post(x, u, wdw, bdw, lng, lnb, w2, b2, seq, tm):
    t, d = x.shape
    nl = d // LANES
    row = lambda a: a.reshape(1, d)
    return pl.pallas_call(
        functools.partial(_conv_post_kernel, tiles_per_seq=seq // tm),
        grid=(t // tm,),
        in_specs=[pl.BlockSpec((tm, d), lambda i: (i, 0)), pl.BlockSpec((nl * tm, LANES), lambda i: (i, 0)),
                  pl.BlockSpec((CONV_WIDTH, nl, LANES), lambda i: (0, 0, 0)), pl.BlockSpec((nl, LANES), lambda i: (0, 0)),
                  pl.BlockSpec((1, d), lambda i: (0, 0)), pl.BlockSpec((1, d), lambda i: (0, 0)),
                  pl.BlockSpec((d, d), lambda i: (0, 0)), pl.BlockSpec((1, d), lambda i: (0, 0))],
        out_specs=pl.BlockSpec((tm, d), lambda i: (i, 0)),
        out_shape=jax.ShapeDtypeStruct((t, d), F32),
        scratch_shapes=[pltpu.VMEM((tm + CONV_HALO, nl, LANES), F32), pltpu.VMEM((nl * tm, LANES), F32)],
        compiler_params=_params("arbitrary"),
        name="conv_post",
    )(x, u, wdw.reshape(CONV_WIDTH, nl, LANES), bdw.reshape(nl, LANES), row(lng), row(lnb), w2.astype(BF16), row(b2))


def _router_kernel(x_ref, g_ref, whi_ref, wlo_ref, slots_ref, gates_ref, counts_ref, carry_ref):
    i = pl.program_id(0)
    tm = x_ref.shape[0]

    @pl.when(i == 0)
    def _():
        carry_ref[...] = jnp.zeros_like(carry_ref)

    h = _rms(x_ref[...], g_ref[...])
    h_hi = h.astype(BF16)
    h_lo = (h - h_hi.astype(F32)).astype(BF16)
    logits = _dot(h_hi, whi_ref[...]) + _dot(h_lo, whi_ref[...]) + _dot(h_hi, wlo_ref[...])
    lane = lax.broadcasted_iota(I32, (tm, LANES), 1)
    lg = jnp.where(lane < N_EXPERTS, logits, -jnp.inf)
    m1 = jnp.max(lg, axis=1, keepdims=True)
    i1 = jnp.min(jnp.where(lg == m1, lane, LANES), axis=1, keepdims=True)
    lg2 = jnp.where(lane == i1, -jnp.inf, lg)
    m2 = jnp.max(lg2, axis=1, keepdims=True)
    i2 = jnp.min(jnp.where(lg2 == m2, lane, LANES), axis=1, keepdims=True)
    e2 = jnp.exp(m2 - m1)
    g0 = 1.0 / (1.0 + e2)
    g1 = e2 / (1.0 + e2)

    oh0 = lane == i1
    oh1 = lane == i2
    member = jnp.where(oh0 | oh1, 1.0, 0.0)
    row = lax.broadcasted_iota(I32, (tm, tm), 0)
    col = lax.broadcasted_iota(I32, (tm, tm), 1)
    before = jnp.where(col < row, 1.0, 0.0).astype(BF16)
    rank = _dot(before, member.astype(BF16)) + carry_ref[...]
    r0 = jnp.sum(jnp.where(oh0, rank, 0.0), axis=1, keepdims=True).astype(I32)
    r1 = jnp.sum(jnp.where(oh1, rank, 0.0), axis=1, keepdims=True).astype(I32)
    carry_ref[...] += jnp.sum(member, axis=0, keepdims=True)
    counts_ref[...] = carry_ref[...]
    slots_ref[...] = jnp.where(lane == 0, r0, jnp.where(lane == 1, r1, jnp.where(lane == 2, i1, jnp.where(lane == 3, i2, 0))))
    gates_ref[...] = jnp.where(lane == 0, g0, jnp.where(lane == 1, g1, 0.0))


def router(x, g, wr, tm=512):
    t, d = x.shape
    tm = min(tm, t)
    wr_p = jnp.zeros((d, LANES), F32).at[:, :N_EXPERTS].set(wr)
    w_hi = wr_p.astype(BF16)
    w_lo = (wr_p - w_hi.astype(F32)).astype(BF16)
    return pl.pallas_call(
        _router_kernel,
        grid=(t // tm,),
        in_specs=[pl.BlockSpec((tm, d), lambda i: (i, 0)), pl.BlockSpec((1, d), lambda i: (0, 0)),
                  pl.BlockSpec((d, LANES), lambda i: (0, 0)), pl.BlockSpec((d, LANES), lambda i: (0, 0))],
        out_specs=[pl.BlockSpec((tm, LANES), lambda i: (i, 0)), pl.BlockSpec((tm, LANES), lambda i: (i, 0)),
                   pl.BlockSpec((1, LANES), lambda i: (0, 0))],
        out_shape=[jax.ShapeDtypeStruct((t, LANES), I32), jax.ShapeDtypeStruct((t, LANES), F32),
                   jax.ShapeDtypeStruct((1, LANES), F32)],
        scratch_shapes=[pltpu.VMEM((1, LANES), F32)],
        compiler_params=_params("arbitrary"),
        name="router",
    )(x, g.reshape(1, d), w_hi, w_lo)


def _dispatch_kernel(fill_ref, slots_ref, x_ref, xs_ref, zero_ref, sem, *, tmf):
    i = pl.program_id(0)
    tm, d = x_ref.shape

    def body(g, carry):
        for sub in range(8):
            r = pl.multiple_of(g * 8, 8) + sub
            for k in range(2):
                s = slots_ref[0, 0, 2 * r + k]
                pltpu.make_async_copy(x_ref.at[pl.ds(r, 1), :], xs_ref.at[pl.ds(s, 1), :], sem).start()
        return carry

    lax.fori_loop(0, tm // 8, body, 0)
    for k in range(2):
        pltpu.make_async_copy(x_ref, xs_ref.at[pl.ds(0, tm), :], sem).wait()

    @pl.when(i == pl.num_programs(0) - 1)
    def _():
        zero_ref[...] = jnp.zeros_like(zero_ref)
        sizes = []
        size = tmf // 2
        while size >= 8:
            sizes.append(size)
            size //= 2

        def zero_row(r, carry):
            row_copy = pltpu.make_async_copy(zero_ref.at[pl.ds(0, 1), :], xs_ref.at[pl.ds(r, 1), :], sem)
            row_copy.start()
            row_copy.wait()
            return carry

        def zero_tile(n, carry):
            tile_copy = pltpu.make_async_copy(zero_ref, xs_ref.at[pl.ds(pl.multiple_of(n * tmf, tmf), tmf), :], sem)
            tile_copy.start()
            tile_copy.wait()
            return carry

        for e in range(N_EXPERTS):
            end = fill_ref[e]
            end8 = (end + 7) // 8 * 8
            lax.fori_loop(end, end8, zero_row, 0)
            rem = fill_ref[N_EXPERTS + e] - end8
            pos = end8
            for size in sizes:
                chunk = pltpu.make_async_copy(zero_ref.at[pl.ds(0, size), :],
                                              xs_ref.at[pl.ds(pl.multiple_of(pos, 8), size), :], sem)
                take = (rem & size) != 0

                @pl.when(take)
                def _():
                    chunk.start()
                    chunk.wait()

                pos = pos + jnp.where(take, size, 0)
        lax.fori_loop(fill_ref[2 * N_EXPERTS], fill_ref[2 * N_EXPERTS + 1], zero_tile, 0)


def dispatch(x, slots_flat, fill, rows, tmf, tm=256):
    t, d = x.shape
    tm = min(tm, t)
    nt = t // tm
    return pl.pallas_call(
        functools.partial(_dispatch_kernel, tmf=tmf),
        grid_spec=pltpu.PrefetchScalarGridSpec(
            num_scalar_prefetch=1,
            grid=(nt,),
            in_specs=[pl.BlockSpec((1, 1, 2 * tm), lambda i, c: (i, 0, 0), memory_space=pltpu.SMEM),
                      pl.BlockSpec((tm, d), lambda i, c: (i, 0))],
            out_specs=pl.BlockSpec(memory_space=pl.ANY),
            scratch_shapes=[pltpu.VMEM((tmf, d), F32), pltpu.SemaphoreType.DMA(())],
        ),
        out_shape=jax.ShapeDtypeStruct((rows, d), F32),
        compiler_params=_params("arbitrary"),
        name="dispatch",
    )(fill, slots_flat.reshape(nt, 1, 2 * tm), x)


def _moe_ffn_kernel(te_ref, na_ref, xs_ref, g_ref, wg_ref, wu_ref, wd_ref, ys_ref, hb_ref):
    i = pl.program_id(0)
    f = pl.program_id(1)

    @pl.when(f == 0)
    def _():
        ys_ref[...] = jnp.zeros_like(ys_ref)

    @pl.when(i < na_ref[0])
    def _():
        @pl.when(f == 0)
        def _():
            hb_ref[...] = _rms(xs_ref[...], g_ref[...]).astype(BF16)

        hb = hb_ref[...]
        a = _dot(hb, wg_ref[...])
        u = _dot(hb, wu_ref[...])
        act = (a * _sigmoid(a) * u).astype(BF16)
        ys_ref[...] += _dot(act, wd_ref[...])


def moe_ffn(xs, g, wg, wu, wd, tile_e, n_act, tmf, tf=512):
    rows, d = xs.shape
    ff = wg.shape[2]
    tf = min(tf, ff)
    nf = ff // tf

    def live(i, na):
        return jnp.minimum(i, na[0] - 1)

    def fidx(i, f, na):
        return jnp.where(i < na[0], f, nf - 1)

    return pl.pallas_call(
        _moe_ffn_kernel,
        grid_spec=pltpu.PrefetchScalarGridSpec(
            num_scalar_prefetch=2,
            grid=(rows // tmf, nf),
            in_specs=[pl.BlockSpec((tmf, d), lambda i, f, te, na: (live(i, na), 0)),
                      pl.BlockSpec((1, d), lambda i, f, te, na: (0, 0)),
                      pl.BlockSpec((None, d, tf), lambda i, f, te, na: (te[live(i, na)], 0, fidx(i, f, na))),
                      pl.BlockSpec((None, d, tf), lambda i, f, te, na: (te[live(i, na)], 0, fidx(i, f, na))),
                      pl.BlockSpec((None, tf, d), lambda i, f, te, na: (te[live(i, na)], fidx(i, f, na), 0))],
            out_specs=pl.BlockSpec((tmf, d), lambda i, f, te, na: (i, 0)),
            scratch_shapes=[pltpu.VMEM((tmf, d), BF16)],
        ),
        out_shape=jax.ShapeDtypeStruct((rows, d), F32),
        compiler_params=_params("arbitrary", "arbitrary"),
        name="moe_ffn",
    )(tile_e, n_act, xs, g.reshape(1, d), wg.astype(BF16), wu.astype(BF16), wd.astype(BF16))


def _combine_kernel(slots_ref, x_ref, gates_ref, ys_ref, gf_ref, out_ref, ybuf_ref, sem, *, final_norm):
    tm, d = x_ref.shape

    def body(g, carry):
        for sub in range(8):
            r = pl.multiple_of(g * 8, 8) + sub
            for k in range(2):
                s = slots_ref[0, 0, 2 * r + k]
                pltpu.make_async_copy(ys_ref.at[pl.ds(s, 1), :], ybuf_ref.at[k, pl.ds(r, 1), :], sem).start()
        return carry

    lax.fori_loop(0, tm // 8, body, 0)
    for k in range(2):
        pltpu.make_async_copy(ys_ref.at[pl.ds(0, tm), :], ybuf_ref.at[k], sem).wait()

    gates = gates_ref[...]
    out = x_ref[...] + gates[:, 0:1] * ybuf_ref[0] + gates[:, 1:2] * ybuf_ref[1]
    if final_norm:
        out = _rms(out, gf_ref[...])
    out_ref[...] = out


def combine(x, slots_flat, gates, ys, g_final, final_norm, tm=256):
    t, d = x.shape
    tm = min(tm, t)
    nt = t // tm
    return pl.pallas_call(
        functools.partial(_combine_kernel, final_norm=final_norm),
        grid=(nt,),
        in_specs=[pl.BlockSpec((1, 1, 2 * tm), lambda i: (i, 0, 0), memory_space=pltpu.SMEM),
                  pl.BlockSpec((tm, d), lambda i: (i, 0)), pl.BlockSpec((tm, LANES), lambda i: (i, 0)),
                  pl.BlockSpec(memory_space=pl.ANY), pl.BlockSpec((1, d), lambda i: (0, 0))],
        out_specs=pl.BlockSpec((tm, d), lambda i: (i, 0)),
        out_shape=jax.ShapeDtypeStruct((t, d), F32),
        scratch_shapes=[pltpu.VMEM((2, tm, d), F32), pltpu.SemaphoreType.DMA(())],
        compiler_params=_params("arbitrary"),
        name="combine",
    )(slots_flat.reshape(nt, 1, 2 * tm), x, gates, ys, g_final.reshape(1, d))


def moe_layer(x, g, wr, wg, wu, wd, g_final, final_norm, tmf=1024):
    t, d = x.shape
    tmf = min(tmf, t)
    meta, gates, counts = router(x, g, wr)
    cnt = counts[0, :N_EXPERTS].astype(I32)

    tiles_e = (cnt + tmf - 1) // tmf
    ends = jnp.cumsum(tiles_e)
    base = (ends - tiles_e) * tmf
    n_max = (2 * t) // tmf + N_EXPERTS
    slots_flat = (base[meta[:, 2:4]] + meta[:, 0:2]).reshape(-1)
    tile_e = jnp.minimum(jnp.sum((jnp.arange(n_max, dtype=I32)[:, None] >= ends[None, :]).astype(I32), axis=1),
                         N_EXPERTS - 1)
    n_act = ends[-1:].astype(I32)
    fill = jnp.concatenate([base + cnt, ends * tmf, n_act, jnp.full((1,), n_max, I32)]).astype(I32)

    xs = dispatch(x, slots_flat, fill, n_max * tmf, tmf)
    ys = moe_ffn(xs, g, wg, wu, wd, tile_e, n_act, tmf)
    return combine(x, slots_flat, gates, ys, g_final, final_norm)


def kernel(x, attn_norm, w_qkv, w_fgate, b_fgate, w_attn_out, conv_norm, w_pw1, b_pw1, w_dw, b_dw,
           conv_ln_g, conv_ln_b, w_pw2, b_pw2, dense_norm, w_dense_gate, w_dense_up, w_dense_down,
           moe_norm, w_router, w_exp_gate, w_exp_up, w_exp_down, final_norm):
    batch, seq, d = x.shape
    depth = attn_norm.shape[0] + conv_norm.shape[0]
    x = x.reshape(batch * seq, d)
    for layer in range(depth):
        j = layer // 2
        if layer % 2 == 0:
            qa, ka, vt = attn_pre(x, attn_norm[j], w_qkv[j], w_fgate[j], b_fgate[j], batch, seq)
            o = flash(qa, ka, vt, batch, seq)
            x = ffn_dense(x, o, w_attn_out[j], dense_norm[j], w_dense_gate[j], w_dense_up[j], w_dense_down[j])
        else:
            u = conv_pre(x, conv_norm[j], w_pw1[j], b_pw1[j], CONV_TILE)
            x = conv_post(x, u, w_dw[j], b_dw[j], conv_ln_g[j], conv_ln_b[j], w_pw2[j], b_pw2[j], seq, CONV_TILE)
            x = moe_layer(x, moe_norm[j], w_router[j], w_exp_gate[j], w_exp_up[j], w_exp_down[j],
                          final_norm, final_norm=(layer == depth - 1))
    return x.reshape(batch, seq, d)
```

```python
import functools

import numpy as np
import jax
import jax.numpy as jnp
from jax import lax
from jax.experimental import pallas as pl
from jax.experimental.pallas import tpu as pltpu

N_HEADS = 16
HEAD_DIM = 64
CONV_WIDTH = 31
N_EXPERTS = 8
EPS = 1e-6

LANES = 128
HEAD_LANES = 128
FLASH_BLOCK = 256
V_ROWS = 80
LOG2E = 1.4426950408889634
CONV_HALO = 32
CONV_TILE = 512
VMEM_LIMIT = 56 * 1024 * 1024
MASK_VALUE = -1e30

F32 = jnp.float32
BF16 = jnp.bfloat16
I32 = jnp.int32


def _params(*sem, flags=None):
    return pltpu.CompilerParams(dimension_semantics=sem, vmem_limit_bytes=VMEM_LIMIT, flags=flags)


def _rms(x, g):
    return x * lax.rsqrt(jnp.mean(x * x, axis=-1, keepdims=True) + EPS) * g


def _sigmoid(x):
    return 1.0 / (1.0 + jnp.exp(-x))


def _dot(a, b):
    return jnp.dot(a, b, preferred_element_type=F32)


def _split3(x):
    hi = x.astype(BF16)
    r1 = x - hi.astype(F32)
    mid = r1.astype(BF16)
    lo = (r1 - mid.astype(F32)).astype(BF16)
    return hi, mid, lo


def _decay_placement():
    eq = np.zeros((LANES, N_HEADS * HEAD_LANES), np.float32)
    ek = np.zeros((LANES, N_HEADS * HEAD_LANES), np.float32)
    for h in range(N_HEADS):
        base = h * HEAD_LANES + HEAD_DIM
        for r in range(3):
            eq[r * N_HEADS + h, base + r] = 1.0
            eq[3 * N_HEADS, base + 3 + r] = 1.0
            ek[3 * N_HEADS, base + r] = 1.0
            ek[r * N_HEADS + h, base + 3 + r] = -1.0
    return jnp.asarray(eq, BF16), jnp.asarray(ek, BF16)


def _attn_pre_kernel(x_ref, g_ref, wqkv_ref, wf_ref, bf_ref, eq_ref, ek_ref,
                     qt_ref, ka_ref, vt_ref, carry_ref, *, tiles_per_seq):
    i = pl.program_id(0)
    tm, d = x_ref.shape

    @pl.when(i % tiles_per_seq == 0)
    def _():
        carry_ref[...] = jnp.zeros_like(carry_ref)

    hb = _rms(x_ref[...], g_ref[...]).astype(BF16)

    z = _dot(hb, wf_ref[...]) + bf_ref[...]
    lf = jnp.minimum(z, 0.0) - jnp.log(1.0 + jnp.exp(-jnp.abs(z)))
    lane = lax.broadcasted_iota(I32, (tm, LANES), 1)
    lf = jnp.where(lane < N_HEADS, lf, 0.0)
    row = lax.broadcasted_iota(I32, (tm, tm), 0)
    col = lax.broadcasted_iota(I32, (tm, tm), 1)
    tri = jnp.where(col <= row, 1.0, 0.0).astype(BF16)
    hi, mid, lo = _split3(lf)
    c = _dot(tri, hi) + _dot(tri, mid) + _dot(tri, lo) + carry_ref[...]
    carry_ref[...] = c[tm - 1:tm, :]

    chi, cmid, clo = _split3(c * LOG2E)
    packed = (chi.astype(F32) + pltpu.roll(cmid.astype(F32), N_HEADS, 1)
              + pltpu.roll(clo.astype(F32), 2 * N_HEADS, 1)
              + jnp.where(lane == 3 * N_HEADS, 1.0, 0.0)).astype(BF16)
    augq = _dot(packed, eq_ref[...])
    augk = _dot(packed, ek_ref[...])

    low = lane < HEAD_DIM
    scale = HEAD_DIM ** -0.5 * LOG2E

    def emit(chunk, aug, head0, mul, transposed):
        for pp in range(chunk.shape[1] // LANES):
            pair = chunk[:, pp * LANES:(pp + 1) * LANES]
            if mul != 1.0:
                pair = pair * mul
            h0 = head0 + 2 * pp
            first = jnp.where(low, pair, aug[:, h0 * HEAD_LANES:(h0 + 1) * HEAD_LANES])
            second = jnp.where(low, pltpu.roll(pair, HEAD_DIM, 1),
                               aug[:, (h0 + 1) * HEAD_LANES:(h0 + 2) * HEAD_LANES])
            if transposed:
                qt_ref[h0 // 2, 0:HEAD_LANES, :] = first.T.astype(BF16)
                qt_ref[h0 // 2, HEAD_LANES:2 * HEAD_LANES, :] = second.T.astype(BF16)
            else:
                ka_ref[:, h0 * HEAD_LANES:(h0 + 1) * HEAD_LANES] = first.astype(BF16)
                ka_ref[:, (h0 + 1) * HEAD_LANES:(h0 + 2) * HEAD_LANES] = second.astype(BF16)

    cw = 256
    tk = vt_ref.shape[-1]
    for cc in range(d // cw):
        heads0 = cc * (cw // HEAD_DIM)
        emit(_dot(hb, wqkv_ref[:, cc * cw:(cc + 1) * cw]), augq, heads0, scale, True)
        emit(_dot(hb, wqkv_ref[:, d + cc * cw:d + (cc + 1) * cw]), augk, heads0, 1.0, False)
        v = _dot(hb, wqkv_ref[:, 2 * d + cc * cw:2 * d + (cc + 1) * cw])
        extra = jnp.where(lax.broadcasted_iota(I32, (V_ROWS - HEAD_DIM, tk), 0) == 0, 1.0, 0.0)
        for pp in range(cw // LANES):
            for jj in range(tm // tk):
                blk = v[jj * tk:(jj + 1) * tk, pp * LANES:(pp + 1) * LANES].T
                rows = jnp.concatenate([blk[:HEAD_DIM], extra, blk[HEAD_DIM:], extra], axis=0)
                vt_ref[cc * (cw // LANES) + pp, jj, :, :] = rows.astype(BF16)


def attn_pre(x, g, wqkv, wf, bf, batch, seq, tm=512):
    t, d = x.shape
    tm = min(tm, seq)
    tk = min(FLASH_BLOCK, seq // 4)
    tiles_per_seq = seq // tm
    eq, ek = _decay_placement()
    wf_p = jnp.zeros((d, LANES), BF16).at[:, :N_HEADS].set(wf.astype(BF16))
    bf_p = jnp.zeros((1, LANES), F32).at[0, :N_HEADS].set(bf)
    na = N_HEADS * HEAD_LANES
    pairs = d // LANES
    full = lambda shape: pl.BlockSpec(shape, lambda i: (0, 0))
    return pl.pallas_call(
        functools.partial(_attn_pre_kernel, tiles_per_seq=tiles_per_seq),
        grid=(t // tm,),
        in_specs=[pl.BlockSpec((tm, d), lambda i: (i, 0)), full((1, d)), full((d, 3 * d)),
                  full((d, LANES)), full((1, LANES)), full((LANES, na)), full((LANES, na))],
        out_specs=[pl.BlockSpec((None, pairs, 2 * HEAD_LANES, tm), lambda i: (i // tiles_per_seq, 0, 0, i % tiles_per_seq)),
                   pl.BlockSpec((tm, na), lambda i: (i, 0)),
                   pl.BlockSpec((None, pairs, tm // tk, 2 * V_ROWS, tk),
                                lambda i: (i // tiles_per_seq, 0, i % tiles_per_seq, 0, 0))],
        out_shape=[jax.ShapeDtypeStruct((batch, pairs, 2 * HEAD_LANES, seq), BF16), jax.ShapeDtypeStruct((t, na), BF16),
                   jax.ShapeDtypeStruct((batch, pairs, seq // tk, 2 * V_ROWS, tk), BF16)],
        scratch_shapes=[pltpu.VMEM((1, LANES), F32)],
        compiler_params=_params("arbitrary"),
        name="attn_pre",
    )(x, g.reshape(1, d), wqkv.astype(BF16), wf_p, bf_p, eq, ek)


def _flash_kernel(qt_ref, ka_ref, vt_ref, o_ref, *scratch, tq, tk):
    qi = pl.program_id(2)
    st_refs, p_refs, al_refs = scratch[0:4], scratch[4:8], scratch[8:12]
    m_ref, acc_ref = scratch[12:]

    def scores(n, slot, col0=0):
        start = pl.multiple_of(n * tk, tk)
        for hh in range(2):
            lanes = slice(hh * HEAD_LANES, (hh + 1) * HEAD_LANES)
            st_refs[slot][hh, :, col0:] = _dot(ka_ref[pl.ds(start, tk), lanes], qt_ref[lanes, col0:])

    def softmax(slot, diagonal, col0=0):
        def load(hh):
            st = st_refs[slot][hh, :, col0:]
            if diagonal:
                key = lax.broadcasted_iota(I32, st.shape, 0)
                qry = lax.broadcasted_iota(I32, st.shape, 1)
                st = jnp.where(key <= qry, st, MASK_VALUE)
            return st

        for hh in range(2):
            m_prev = m_ref[hh, :, col0:]
            m_new = jnp.maximum(m_prev, jnp.max(load(hh), axis=0, keepdims=True))
            m_ref[hh, :, col0:] = m_new
            al_refs[slot][hh, :, col0:] = jnp.exp2(m_prev - m_new)
            p_refs[slot][hh, :, col0:] = jnp.exp2((load(hh) - m_new).astype(BF16))

    def values(n, slot, col0=0):
        for hh in range(2):
            vt = vt_ref[n, hh * V_ROWS:(hh + 1) * V_ROWS, :]
            acc_ref[hh, :, col0:] = (al_refs[slot][hh, :, col0:] * acc_ref[hh, :, col0:]
                                     + _dot(vt, p_refs[slot][hh, :, col0:]))

    def trip(n, diagonal):
        col = [slot * tk if diagonal else 0 for slot in range(4)]
        scores(n + 2, 2, col[2])
        scores(n + 3, 3, col[3])
        values(jnp.maximum(n - 2, 0), 2)
        values(jnp.maximum(n - 1, 0), 3)
        softmax(0, diagonal, col[0])
        softmax(1, diagonal, col[1])
        if not diagonal:
            scores(n + 4, 0)
            scores(n + 5, 1)
        values(n, 0, col[0])
        values(n + 1, 1, col[1])
        softmax(2, diagonal, col[2])
        softmax(3, diagonal, col[3])

    m_ref[...] = jnp.full_like(m_ref, MASK_VALUE)
    acc_ref[...] = jnp.zeros_like(acc_ref)
    for slot in (2, 3):
        p_refs[slot][...] = jnp.zeros_like(p_refs[slot])
        al_refs[slot][...] = jnp.ones_like(al_refs[slot])
    scores(0, 0)
    scores(1, 1)

    def body(t, carry):
        trip(4 * t, False)
        return carry

    lax.fori_loop(0, qi, body, 0)
    trip(4 * qi, True)
    values(4 * qi + 2, 2, 2 * tk)
    values(4 * qi + 3, 3, 3 * tk)
    ot = jnp.concatenate([acc_ref[hh, :HEAD_DIM, :] / acc_ref[hh, HEAD_DIM:HEAD_DIM + 1, :] for hh in range(2)],
                         axis=0)
    o_ref[...] = ot.T.astype(BF16)


def flash(qt, ka, vt, batch, seq):
    t = ka.shape[0]
    tk = vt.shape[-1]
    tq = 4 * tk
    nq = seq // tq
    pairs = N_HEADS // 2
    return pl.pallas_call(
        functools.partial(_flash_kernel, tq=tq, tk=tk),
        grid=(batch, pairs, nq),
        in_specs=[pl.BlockSpec((None, None, 2 * HEAD_LANES, tq), lambda b, p, q: (b, p, 0, q)),
                  pl.BlockSpec((seq, 2 * HEAD_LANES), lambda b, p, q: (b, p)),
                  pl.BlockSpec((None, None, seq // tk, 2 * V_ROWS, tk), lambda b, p, q: (b, p, 0, 0, 0))],
        out_specs=pl.BlockSpec((tq, LANES), lambda b, p, q: (b * nq + q, p)),
        out_shape=jax.ShapeDtypeStruct((t, pairs * LANES), BF16),
        scratch_shapes=([pltpu.VMEM((2, tk, tq), F32)] * 4 + [pltpu.VMEM((2, tk, tq), BF16)] * 4
                        + [pltpu.VMEM((2, 1, tq), F32)] * 4
                        + [pltpu.VMEM((2, 1, tq), F32), pltpu.VMEM((2, V_ROWS, tq), F32)]),
        compiler_params=_params("arbitrary", "arbitrary", "arbitrary"),
        name="flash",
    )(qt, ka, vt)


def _ffn_kernel(x_ref, o_ref, wo_ref, g_ref, wg_ref, wu_ref, wd_ref, out_ref, hb_ref):
    f = pl.program_id(1)

    @pl.when(f == 0)
    def _():
        x1 = x_ref[...] + _dot(o_ref[...], wo_ref[...])
        out_ref[...] = x1
        hb_ref[...] = _rms(x1, g_ref[...]).astype(BF16)

    hb = hb_ref[...]
    a = _dot(hb, wg_ref[...])
    u = _dot(hb, wu_ref[...])
    act = (a * _sigmoid(a) * u).astype(BF16)
    out_ref[...] += _dot(act, wd_ref[...])


def ffn_dense(x, o, wo, g, wg, wu, wd, tm=1024, tf=512):
    t, d = x.shape
    ff = wg.shape[1]
    tm = min(tm, t)
    tf = min(tf, ff)
    return pl.pallas_call(
        _ffn_kernel,
        grid=(t // tm, ff // tf),
        in_specs=[pl.BlockSpec((tm, d), lambda i, f: (i, 0)), pl.BlockSpec((tm, d), lambda i, f: (i, 0)),
                  pl.BlockSpec((d, d), lambda i, f: (0, 0)), pl.BlockSpec((1, d), lambda i, f: (0, 0)),
                  pl.BlockSpec((d, tf), lambda i, f: (0, f)), pl.BlockSpec((d, tf), lambda i, f: (0, f)),
                  pl.BlockSpec((tf, d), lambda i, f: (f, 0))],
        out_specs=pl.BlockSpec((tm, d), lambda i, f: (i, 0)),
        out_shape=jax.ShapeDtypeStruct((t, d), F32),
        scratch_shapes=[pltpu.VMEM((tm, d), BF16)],
        compiler_params=_params("arbitrary", "arbitrary"),
        name="ffn_dense",
    )(x, o, wo.astype(BF16), g.reshape(1, d), wg.astype(BF16), wu.astype(BF16), wd.astype(BF16))


def _conv_pre_kernel(x_ref, g_ref, w1_ref, b1_ref, u_ref):
    tm, d = x_ref.shape
    hb = _rms(x_ref[...], g_ref[...]).astype(BF16)
    cw = 256
    for cc in range(d // cw):
        lo, hi = cc * cw, (cc + 1) * cw
        a = _dot(hb, w1_ref[:, lo:hi]) + b1_ref[:, lo:hi]
        b = _dot(hb, w1_ref[:, d + lo:d + hi]) + b1_ref[:, d + lo:d + hi]
        u = a * _sigmoid(b)
        for kk in range(cw // LANES):
            k = cc * (cw // LANES) + kk
            u_ref[k * tm:(k + 1) * tm, :] = u[:, kk * LANES:(kk + 1) * LANES]


def conv_pre(x, g, w1, b1, tm):
    t, d = x.shape
    nl = d // LANES
    return pl.pallas_call(
        _conv_pre_kernel,
        grid=(t // tm,),
        in_specs=[pl.BlockSpec((tm, d), lambda i: (i, 0)), pl.BlockSpec((1, d), lambda i: (0, 0)),
                  pl.BlockSpec((d, 2 * d), lambda i: (0, 0)), pl.BlockSpec((1, 2 * d), lambda i: (0, 0))],
        out_specs=pl.BlockSpec((nl * tm, LANES), lambda i: (i, 0)),
        out_shape=jax.ShapeDtypeStruct((t * nl, LANES), F32),
        compiler_params=_params("arbitrary"),
        name="conv_pre",
    )(x, g.reshape(1, d), w1.astype(BF16), b1.reshape(1, 2 * d))


def _conv_post_kernel(x_ref, u_ref, wdw_ref, bdw_ref, lng_ref, lnb_ref, w2_ref, b2_ref,
                      out_ref, time_ref, ystack_ref, *, tiles_per_seq):
    i = pl.program_id(0)
    tm, d = x_ref.shape
    nl = d // LANES
    group = 16
    n_groups = tm // group

    @pl.when(i % tiles_per_seq == 0)
    def _():
        time_ref[0:CONV_HALO] = jnp.zeros((CONV_HALO, nl, LANES), F32)

    def to_time_major(r, carry):
        time_ref[CONV_HALO + r] = u_ref[pl.ds(r, nl, stride=tm), :]
        return carry

    lax.fori_loop(0, tm, to_time_major, 0, unroll=8)

    shift = CONV_HALO - (CONV_WIDTH - 1)

    def conv(g, carry):
        base = g * group
        accs = [bdw_ref[...]] * group
        for j in range(CONV_WIDTH):
            w = wdw_ref[j]
            accs = [accs[tt] + w * time_ref[base + shift + j + tt] for tt in range(group)]
        for tt in range(group):
            ystack_ref[pl.ds(base + tt, nl, stride=tm), :] = accs[tt]
        return carry

    lax.fori_loop(0, n_groups, conv, 0)
    time_ref[0:CONV_HALO] = time_ref[tm:tm + CONV_HALO]

    y = jnp.concatenate([ystack_ref[k * tm:(k + 1) * tm, :] for k in range(nl)], axis=1)
    mu = jnp.mean(y, axis=-1, keepdims=True)
    yc = y - mu
    z = yc * lax.rsqrt(jnp.mean(yc * yc, axis=-1, keepdims=True) + EPS) * lng_ref[...] + lnb_ref[...]
    zb = (z * _sigmoid(z)).astype(BF16)
    out_ref[...] = x_ref[...] + _dot(zb, w2_ref[...]) + b2_ref[...]


def conv_post(x, u, wdw, bdw, lng, lnb, w2, b2, seq, tm):
    t, d = x.shape
    nl = d // LANES
    row = lambda a: a.reshape(1, d)
    return pl.pallas_call(
        functools.partial(_conv_post_kernel, tiles_per_seq=seq // tm),
        grid=(t // tm,),
        in_specs=[pl.BlockSpec((tm, d), lambda i: (i, 0)), pl.BlockSpec((nl * tm, LANES), lambda i: (i, 0)),
                  pl.BlockSpec((CONV_WIDTH, nl, LANES), lambda i: (0, 0, 0)), pl.BlockSpec((nl, LANES), lambda i: (0, 0)),
                  pl.BlockSpec((1, d), lambda i: (0, 0)), pl.BlockSpec((1, d), lambda i: (0, 0)),
                  pl.BlockSpec((d, d), lambda i: (0, 0)), pl.BlockSpec((1, d), lambda i: (0, 0))],
        out_specs=pl.BlockSpec((tm, d), lambda i: (i, 0)),
        out_shape=jax.ShapeDtypeStruct((t, d), F32),
        scratch_shapes=[pltpu.VMEM((tm + CONV_HALO, nl, LANES), F32), pltpu.VMEM((nl * tm, LANES), F32)],
        compiler_params=_params("arbitrary"),
        name="conv_post",
    )(x, u, wdw.reshape(CONV_WIDTH, nl, LANES), bdw.reshape(nl, LANES), row(lng), row(lnb), w2.astype(BF16), row(b2))


def _router_kernel(x_ref, g_ref, whi_ref, wlo_ref, slots_ref, gates_ref, counts_ref, carry_ref):
    i = pl.program_id(0)
    tm = x_ref.shape[0]

    @pl.when(i == 0)
    def _():
        carry_ref[...] = jnp.zeros_like(carry_ref)

    h = _rms(x_ref[...], g_ref[...])
    h_hi = h.astype(BF16)
    h_lo = (h - h_hi.astype(F32)).astype(BF16)
    logits = _dot(h_hi, whi_ref[...]) + _dot(h_lo, whi_ref[...]) + _dot(h_hi, wlo_ref[...])
    lane = lax.broadcasted_iota(I32, (tm, LANES), 1)
    lg = jnp.where(lane < N_EXPERTS, logits, -jnp.inf)
    m1 = jnp.max(lg, axis=1, keepdims=True)
    i1 = jnp.min(jnp.where(lg == m1, lane, LANES), axis=1, keepdims=True)
    lg2 = jnp.where(lane == i1, -jnp.inf, lg)
    m2 = jnp.max(lg2, axis=1, keepdims=True)
    i2 = jnp.min(jnp.where(lg2 == m2, lane, LANES), axis=1, keepdims=True)
    e2 = jnp.exp(m2 - m1)
    g0 = 1.0 / (1.0 + e2)
    g1 = e2 / (1.0 + e2)

    oh0 = lane == i1
    oh1 = lane == i2
    member = jnp.where(oh0 | oh1, 1.0, 0.0)
    row = lax.broadcasted_iota(I32, (tm, tm), 0)
    col = lax.broadcasted_iota(I32, (tm, tm), 1)
    before = jnp.where(col < row, 1.0, 0.0).astype(BF16)
    rank = _dot(before, member.astype(BF16)) + carry_ref[...]
    r0 = jnp.sum(jnp.where(oh0, rank, 0.0), axis=1, keepdims=True).astype(I32)
    r1 = jnp.sum(jnp.where(oh1, rank, 0.0), axis=1, keepdims=True).astype(I32)
    carry_ref[...] += jnp.sum(member, axis=0, keepdims=True)
    counts_ref[...] = carry_ref[...]
    slots_ref[...] = jnp.where(lane == 0, r0, jnp.where(lane == 1, r1, jnp.where(lane == 2, i1, jnp.where(lane == 3, i2, 0))))
    gates_ref[...] = jnp.where(lane == 0, g0, jnp.where(lane == 1, g1, 0.0))


def router(x, g, wr, tm=512):
    t, d = x.shape
    tm = min(tm, t)
    wr_p = jnp.zeros((d, LANES), F32).at[:, :N_EXPERTS].set(wr)
    w_hi = wr_p.astype(BF16)
    w_lo = (wr_p - w_hi.astype(F32)).astype(BF16)
    return pl.pallas_call(
        _router_kernel,
        grid=(t // tm,),
        in_specs=[pl.BlockSpec((tm, d), lambda i: (i, 0)), pl.BlockSpec((1, d), lambda i: (0, 0)),
                  pl.BlockSpec((d, LANES), lambda i: (0, 0)), pl.BlockSpec((d, LANES), lambda i: (0, 0))],
        out_specs=[pl.BlockSpec((tm, LANES), lambda i: (i, 0)), pl.BlockSpec((tm, LANES), lambda i: (i, 0)),
                   pl.BlockSpec((1, LANES), lambda i: (0, 0))],
        out_shape=[jax.ShapeDtypeStruct((t, LANES), I32), jax.ShapeDtypeStruct((t, LANES), F32),
                   jax.ShapeDtypeStruct((1, LANES), F32)],
        scratch_shapes=[pltpu.VMEM((1, LANES), F32)],
        compiler_params=_params("arbitrary"),
        name="router",
    )(x, g.reshape(1, d), w_hi, w_lo)


def _dispatch_kernel(fill_ref, slots_ref, x_ref, xs_ref, zero_ref, sem, *, tmf):
    i = pl.program_id(0)
    tm, d = x_ref.shape

    def body(g, carry):
        for sub in range(8):
            r = pl.multiple_of(g * 8, 8) + sub
            for k in range(2):
                s = slots_ref[0, 0, 2 * r + k]
                pltpu.make_async_copy(x_ref.at[pl.ds(r, 1), :], xs_ref.at[pl.ds(s, 1), :], sem).start()
        return carry

    lax.fori_loop(0, tm // 8, body, 0)
    for k in range(2):
        pltpu.make_async_copy(x_ref, xs_ref.at[pl.ds(0, tm), :], sem).wait()

    @pl.when(i == pl.num_programs(0) - 1)
    def _():
        zero_ref[...] = jnp.zeros_like(zero_ref)
        sizes = []
        size = tmf // 2
        while size >= 8:
            sizes.append(size)
            size //= 2

        def zero_row(r, carry):
            row_copy = pltpu.make_async_copy(zero_ref.at[pl.ds(0, 1), :], xs_ref.at[pl.ds(r, 1), :], sem)
            row_copy.start()
            row_copy.wait()
            return carry

        def zero_tile(n, carry):
            tile_copy = pltpu.make_async_copy(zero_ref, xs_ref.at[pl.ds(pl.multiple_of(n * tmf, tmf), tmf), :], sem)
            tile_copy.start()
            tile_copy.wait()
            return carry

        for e in range(N_EXPERTS):
            end = fill_ref[e]
            end8 = (end + 7) // 8 * 8
            lax.fori_loop(end, end8, zero_row, 0)
            rem = fill_ref[N_EXPERTS + e] - end8
            pos = end8
            for size in sizes:
                chunk = pltpu.make_async_copy(zero_ref.at[pl.ds(0, size), :],
                                              xs_ref.at[pl.ds(pl.multiple_of(pos, 8), size), :], sem)
                take = (rem & size) != 0

                @pl.when(take)
                def _():
                    chunk.start()
                    chunk.wait()

                pos = pos + jnp.where(take, size, 0)
        lax.fori_loop(fill_ref[2 * N_EXPERTS], fill_ref[2 * N_EXPERTS + 1], zero_tile, 0)


def dispatch(x, slots_flat, fill, rows, tmf, tm=256):
    t, d = x.shape
    tm = min(tm, t)
    nt = t // tm
    return pl.pallas_call(
        functools.partial(_dispatch_kernel, tmf=tmf),
        grid_spec=pltpu.PrefetchScalarGridSpec(
            num_scalar_prefetch=1,
            grid=(nt,),
            in_specs=[pl.BlockSpec((1, 1, 2 * tm), lambda i, c: (i, 0, 0), memory_space=pltpu.SMEM),
                      pl.BlockSpec((tm, d), lambda i, c: (i, 0))],
            out_specs=pl.BlockSpec(memory_space=pl.ANY),
            scratch_shapes=[pltpu.VMEM((tmf, d), F32), pltpu.SemaphoreType.DMA(())],
        ),
        out_shape=jax.ShapeDtypeStruct((rows, d), F32),
        compiler_params=_params("arbitrary"),
        name="dispatch",
    )(fill, slots_flat.reshape(nt, 1, 2 * tm), x)


def _moe_ffn_kernel(te_ref, na_ref, xs_ref, g_ref, wg_ref, wu_ref, wd_ref, ys_ref, hb_ref):
    i = pl.program_id(0)
    f = pl.program_id(1)

    @pl.when(f == 0)
    def _():
        ys_ref[...] = jnp.zeros_like(ys_ref)

    @pl.when(i < na_ref[0])
    def _():
        @pl.when(f == 0)
        def _():
            hb_ref[...] = _rms(xs_ref[...], g_ref[...]).astype(BF16)

        hb = hb_ref[...]
        a = _dot(hb, wg_ref[...])
        u = _dot(hb, wu_ref[...])
        act = (a * _sigmoid(a) * u).astype(BF16)
        ys_ref[...] += _dot(act, wd_ref[...])


def moe_ffn(xs, g, wg, wu, wd, tile_e, n_act, tmf, tf=512):
    rows, d = xs.shape
    ff = wg.shape[2]
    tf = min(tf, ff)
    nf = ff // tf

    def live(i, na):
        return jnp.minimum(i, na[0] - 1)

    def fidx(i, f, na):
        return jnp.where(i < na[0], f, nf - 1)

    return pl.pallas_call(
        _moe_ffn_kernel,
        grid_spec=pltpu.PrefetchScalarGridSpec(
            num_scalar_prefetch=2,
            grid=(rows // tmf, nf),
            in_specs=[pl.BlockSpec((tmf, d), lambda i, f, te, na: (live(i, na), 0)),
                      pl.BlockSpec((1, d), lambda i, f, te, na: (0, 0)),
                      pl.BlockSpec((None, d, tf), lambda i, f, te, na: (te[live(i, na)], 0, fidx(i, f, na))),
                      pl.BlockSpec((None, d, tf), lambda i, f, te, na: (te[live(i, na)], 0, fidx(i, f, na))),
                      pl.BlockSpec((None, tf, d), lambda i, f, te, na: (te[live(i, na)], fidx(i, f, na), 0))],
            out_specs=pl.BlockSpec((tmf, d), lambda i, f, te, na: (i, 0)),
            scratch_shapes=[pltpu.VMEM((tmf, d), BF16)],
        ),
        out_shape=jax.ShapeDtypeStruct((rows, d), F32),
        compiler_params=_params("arbitrary", "arbitrary"),
        name="moe_ffn",
    )(tile_e, n_act, xs, g.reshape(1, d), wg.astype(BF16), wu.astype(BF16), wd.astype(BF16))


def _combine_kernel(slots_ref, x_ref, gates_ref, ys_ref, gf_ref, out_ref, ybuf_ref, sem, *, final_norm):
    tm, d = x_ref.shape

    def body(g, carry):
        for sub in range(8):
            r = pl.multiple_of(g * 8, 8) + sub
            for k in range(2):
                s = slots_ref[0, 0, 2 * r + k]
                pltpu.make_async_copy(ys_ref.at[pl.ds(s, 1), :], ybuf_ref.at[k, pl.ds(r, 1), :], sem).start()
        return carry

    lax.fori_loop(0, tm // 8, body, 0)
    for k in range(2):
        pltpu.make_async_copy(ys_ref.at[pl.ds(0, tm), :], ybuf_ref.at[k], sem).wait()

    gates = gates_ref[...]
    out = x_ref[...] + gates[:, 0:1] * ybuf_ref[0] + gates[:, 1:2] * ybuf_ref[1]
    if final_norm:
        out = _rms(out, gf_ref[...])
    out_ref[...] = out


def combine(x, slots_flat, gates, ys, g_final, final_norm, tm=256):
    t, d = x.shape
    tm = min(tm, t)
    nt = t // tm
    return pl.pallas_call(
        functools.partial(_combine_kernel, final_norm=final_norm),
        grid=(nt,),
        in_specs=[pl.BlockSpec((1, 1, 2 * tm), lambda i: (i, 0, 0), memory_space=pltpu.SMEM),
                  pl.BlockSpec((tm, d), lambda i: (i, 0)), pl.BlockSpec((tm, LANES), lambda i: (i, 0)),
                  pl.BlockSpec(memory_space=pl.ANY), pl.BlockSpec((1, d), lambda i: (0, 0))],
        out_specs=pl.BlockSpec((tm, d), lambda i: (i, 0)),
        out_shape=jax.ShapeDtypeStruct((t, d), F32),
        scratch_shapes=[pltpu.VMEM((2, tm, d), F32), pltpu.SemaphoreType.DMA(())],
        compiler_params=_params("arbitrary"),
        name="combine",
    )(slots_flat.reshape(nt, 1, 2 * tm), x, gates, ys, g_final.reshape(1, d))


def moe_layer(x, g, wr, wg, wu, wd, g_final, final_norm, tmf=1024):
    t, d = x.shape
    tmf = min(tmf, t)
    meta, gates, counts = router(x, g, wr)
    cnt = counts[0, :N_EXPERTS].astype(I32)

    tiles_e = (cnt + tmf - 1) // tmf
    ends = jnp.cumsum(tiles_e)
    base = (ends - tiles_e) * tmf
    n_max = (2 * t) // tmf + N_EXPERTS
    slots_flat = (base[meta[:, 2:4]] + meta[:, 0:2]).reshape(-1)
    tile_e = jnp.minimum(jnp.sum((jnp.arange(n_max, dtype=I32)[:, None] >= ends[None, :]).astype(I32), axis=1),
                         N_EXPERTS - 1)
    n_act = ends[-1:].astype(I32)
    fill = jnp.concatenate([base + cnt, ends * tmf, n_act, jnp.full((1,), n_max, I32)]).astype(I32)

    xs = dispatch(x, slots_flat, fill, n_max * tmf, tmf)
    ys = moe_ffn(xs, g, wg, wu, wd, tile_e, n_act, tmf)
    return combine(x, slots_flat, gates, ys, g_final, final_norm)


def kernel(x, attn_norm, w_qkv, w_fgate, b_fgate, w_attn_out, conv_norm, w_pw1, b_pw1, w_dw, b_dw,
           conv_ln_g, conv_ln_b, w_pw2, b_pw2, dense_norm, w_dense_gate, w_dense_up, w_dense_down,
           moe_norm, w_router, w_exp_gate, w_exp_up, w_exp_down, final_norm):
    batch, seq, d = x.shape
    depth = attn_norm.shape[0] + conv_norm.shape[0]
    x = x.reshape(batch * seq, d)
    for layer in range(depth):
        j = layer // 2
        if layer % 2 == 0:
            qt, ka, vt = attn_pre(x, attn_norm[j], w_qkv[j], w_fgate[j], b_fgate[j], batch, seq)
            o = flash(qt, ka, vt, batch, seq)
            x = ffn_dense(x, o, w_attn_out[j], dense_norm[j], w_dense_gate[j], w_dense_up[j], w_dense_down[j])
        else:
            u = conv_pre(x, conv_norm[j], w_pw1[j], b_pw1[j], CONV_TILE)
            x = conv_post(x, u, w_dw[j], b_dw[j], conv_ln_g[j], conv_ln_b[j], w_pw2[j], b_pw2[j], seq, CONV_TILE)
            x = moe_layer(x, moe_norm[j], w_router[j], w_exp_gate[j], w_exp_up[j], w_exp_down[j],
                          final_norm, final_norm=(layer == depth - 1))
    return x.reshape(batch, seq, d)
```

```python
import functools

import numpy as np
import jax
import jax.numpy as jnp
from jax import lax
from jax.experimental import pallas as pl
from jax.experimental.pallas import tpu as pltpu

N_HEADS = 16
HEAD_DIM = 64
CONV_WIDTH = 31
N_EXPERTS = 8
EPS = 1e-6

LANES = 128
HEAD_LANES = 128
FLASH_BLOCK = 256
V_ROWS = 80
LOG2E = 1.4426950408889634
CONV_HALO = 32
CONV_TILE = 512
VMEM_LIMIT = 56 * 1024 * 1024
MASK_VALUE = -1e30

F32 = jnp.float32
BF16 = jnp.bfloat16
I32 = jnp.int32


def _params(*sem, flags=None):
    return pltpu.CompilerParams(dimension_semantics=sem, vmem_limit_bytes=VMEM_LIMIT, flags=flags)


def _rms(x, g):
    return x * lax.rsqrt(jnp.mean(x * x, axis=-1, keepdims=True) + EPS) * g


def _sigmoid(x):
    return 1.0 / (1.0 + jnp.exp(-x))


def _dot(a, b):
    return jnp.dot(a, b, preferred_element_type=F32)


def _split3(x):
    hi = x.astype(BF16)
    r1 = x - hi.astype(F32)
    mid = r1.astype(BF16)
    lo = (r1 - mid.astype(F32)).astype(BF16)
    return hi, mid, lo


def _decay_placement():
    eq = np.zeros((LANES, N_HEADS * HEAD_LANES), np.float32)
    ek = np.zeros((LANES, N_HEADS * HEAD_LANES), np.float32)
    for h in range(N_HEADS):
        base = h * HEAD_LANES + HEAD_DIM
        for r in range(3):
            eq[r * N_HEADS + h, base + r] = 1.0
            eq[3 * N_HEADS, base + 3 + r] = 1.0
            ek[3 * N_HEADS, base + r] = 1.0
            ek[r * N_HEADS + h, base + 3 + r] = -1.0
    return jnp.asarray(eq, BF16), jnp.asarray(ek, BF16)


def _attn_pre_kernel(x_ref, g_ref, wqkv_ref, wf_ref, bf_ref, eq_ref, ek_ref,
                     qa_ref, ka_ref, vt_ref, carry_ref, *, tiles_per_seq):
    i = pl.program_id(0)
    tm, d = x_ref.shape

    @pl.when(i % tiles_per_seq == 0)
    def _():
        carry_ref[...] = jnp.zeros_like(carry_ref)

    hb = _rms(x_ref[...], g_ref[...]).astype(BF16)

    z = _dot(hb, wf_ref[...]) + bf_ref[...]
    lf = jnp.minimum(z, 0.0) - jnp.log(1.0 + jnp.exp(-jnp.abs(z)))
    lane = lax.broadcasted_iota(I32, (tm, LANES), 1)
    lf = jnp.where(lane < N_HEADS, lf, 0.0)
    row = lax.broadcasted_iota(I32, (tm, tm), 0)
    col = lax.broadcasted_iota(I32, (tm, tm), 1)
    tri = jnp.where(col <= row, 1.0, 0.0).astype(BF16)
    hi, mid, lo = _split3(lf)
    c = _dot(tri, hi) + _dot(tri, mid) + _dot(tri, lo) + carry_ref[...]
    carry_ref[...] = c[tm - 1:tm, :]

    chi, cmid, clo = _split3(c * LOG2E)
    packed = (chi.astype(F32) + pltpu.roll(cmid.astype(F32), N_HEADS, 1)
              + pltpu.roll(clo.astype(F32), 2 * N_HEADS, 1)
              + jnp.where(lane == 3 * N_HEADS, 1.0, 0.0)).astype(BF16)
    augq = _dot(packed, eq_ref[...])
    augk = _dot(packed, ek_ref[...])

    low = lane < HEAD_DIM
    scale = HEAD_DIM ** -0.5 * LOG2E

    def emit(dst_ref, chunk, aug, head0, mul):
        for pp in range(chunk.shape[1] // LANES):
            pair = chunk[:, pp * LANES:(pp + 1) * LANES]
            if mul != 1.0:
                pair = pair * mul
            h0 = head0 + 2 * pp
            first = jnp.where(low, pair, aug[:, h0 * HEAD_LANES:(h0 + 1) * HEAD_LANES])
            second = jnp.where(low, pltpu.roll(pair, HEAD_DIM, 1),
                               aug[:, (h0 + 1) * HEAD_LANES:(h0 + 2) * HEAD_LANES])
            dst_ref[:, h0 * HEAD_LANES:(h0 + 1) * HEAD_LANES] = first.astype(BF16)
            dst_ref[:, (h0 + 1) * HEAD_LANES:(h0 + 2) * HEAD_LANES] = second.astype(BF16)

    cw = 256
    tk = vt_ref.shape[-1]
    for cc in range(d // cw):
        heads0 = cc * (cw // HEAD_DIM)
        emit(qa_ref, _dot(hb, wqkv_ref[:, cc * cw:(cc + 1) * cw]), augq, heads0, scale)
        emit(ka_ref, _dot(hb, wqkv_ref[:, d + cc * cw:d + (cc + 1) * cw]), augk, heads0, 1.0)
        v = _dot(hb, wqkv_ref[:, 2 * d + cc * cw:2 * d + (cc + 1) * cw])
        extra = jnp.where(lax.broadcasted_iota(I32, (V_ROWS - HEAD_DIM, tk), 0) == 0, 1.0, 0.0)
        for pp in range(cw // LANES):
            for jj in range(tm // tk):
                blk = v[jj * tk:(jj + 1) * tk, pp * LANES:(pp + 1) * LANES].T
                rows = jnp.concatenate([blk[:HEAD_DIM], extra, blk[HEAD_DIM:], extra], axis=0)
                vt_ref[cc * (cw // LANES) + pp, jj, :, :] = rows.astype(BF16)


def attn_pre(x, g, wqkv, wf, bf, batch, seq, tm=512):
    t, d = x.shape
    tm = min(tm, seq)
    tk = min(FLASH_BLOCK, seq // 4)
    tiles_per_seq = seq // tm
    eq, ek = _decay_placement()
    wf_p = jnp.zeros((d, LANES), BF16).at[:, :N_HEADS].set(wf.astype(BF16))
    bf_p = jnp.zeros((1, LANES), F32).at[0, :N_HEADS].set(bf)
    na = N_HEADS * HEAD_LANES
    pairs = d // LANES
    full = lambda shape: pl.BlockSpec(shape, lambda i: (0, 0))
    return pl.pallas_call(
        functools.partial(_attn_pre_kernel, tiles_per_seq=tiles_per_seq),
        grid=(t // tm,),
        in_specs=[pl.BlockSpec((tm, d), lambda i: (i, 0)), full((1, d)), full((d, 3 * d)),
                  full((d, LANES)), full((1, LANES)), full((LANES, na)), full((LANES, na))],
        out_specs=[pl.BlockSpec((tm, na), lambda i: (i, 0)), pl.BlockSpec((tm, na), lambda i: (i, 0)),
                   pl.BlockSpec((None, pairs, tm // tk, 2 * V_ROWS, tk),
                                lambda i: (i // tiles_per_seq, 0, i % tiles_per_seq, 0, 0))],
        out_shape=[jax.ShapeDtypeStruct((t, na), BF16), jax.ShapeDtypeStruct((t, na), BF16),
                   jax.ShapeDtypeStruct((batch, pairs, seq // tk, 2 * V_ROWS, tk), BF16)],
        scratch_shapes=[pltpu.VMEM((1, LANES), F32)],
        compiler_params=_params("arbitrary"),
        name="attn_pre",
    )(x, g.reshape(1, d), wqkv.astype(BF16), wf_p, bf_p, eq, ek)


def _flash_kernel(qa_ref, ka_ref, vt_ref, o_ref, *scratch, tq, tk):
    qi = pl.program_id(2)
    st_refs, p_refs, al_refs = scratch[0:4], scratch[4:8], scratch[8:12]
    m_ref, acc_ref = scratch[12:]

    def scores(n, slot, col0=0):
        start = pl.multiple_of(n * tk, tk)
        for hh in range(2):
            lanes = slice(hh * HEAD_LANES, (hh + 1) * HEAD_LANES)
            st_refs[slot][hh, :, col0:] = lax.dot_general(ka_ref[pl.ds(start, tk), lanes], qa_ref[col0:, lanes],
                                                          (((1,), (1,)), ((), ())), preferred_element_type=F32)

    def softmax(slot, diagonal, col0=0):
        def load(hh):
            st = st_refs[slot][hh, :, col0:]
            if diagonal:
                key = lax.broadcasted_iota(I32, st.shape, 0)
                qry = lax.broadcasted_iota(I32, st.shape, 1)
                st = jnp.where(key <= qry, st, MASK_VALUE)
            return st

        for hh in range(2):
            m_prev = m_ref[hh, :, col0:]
            m_new = jnp.maximum(m_prev, jnp.max(load(hh), axis=0, keepdims=True))
            m_ref[hh, :, col0:] = m_new
            al_refs[slot][hh, :, col0:] = jnp.exp2(m_prev - m_new)
            p_refs[slot][hh, :, col0:] = jnp.exp2((load(hh) - m_new).astype(BF16))

    def values(n, slot, col0=0):
        for hh in range(2):
            vt = vt_ref[n, hh * V_ROWS:(hh + 1) * V_ROWS, :]
            acc_ref[hh, :, col0:] = (al_refs[slot][hh, :, col0:] * acc_ref[hh, :, col0:]
                                     + _dot(vt, p_refs[slot][hh, :, col0:]))

    def trip(n, diagonal):
        col = [slot * tk if diagonal else 0 for slot in range(4)]
        scores(n + 2, 2, col[2])
        scores(n + 3, 3, col[3])
        values(jnp.maximum(n - 2, 0), 2)
        values(jnp.maximum(n - 1, 0), 3)
        softmax(0, diagonal, col[0])
        softmax(1, diagonal, col[1])
        if not diagonal:
            scores(n + 4, 0)
            scores(n + 5, 1)
        values(n, 0, col[0])
        values(n + 1, 1, col[1])
        softmax(2, diagonal, col[2])
        softmax(3, diagonal, col[3])

    m_ref[...] = jnp.full_like(m_ref, MASK_VALUE)
    acc_ref[...] = jnp.zeros_like(acc_ref)
    for slot in (2, 3):
        p_refs[slot][...] = jnp.zeros_like(p_refs[slot])
        al_refs[slot][...] = jnp.ones_like(al_refs[slot])
    scores(0, 0)
    scores(1, 1)

    def body(t, carry):
        trip(4 * t, False)
        return carry

    lax.fori_loop(0, qi, body, 0)
    trip(4 * qi, True)
    values(4 * qi + 2, 2, 2 * tk)
    values(4 * qi + 3, 3, 3 * tk)
    ot = jnp.concatenate([acc_ref[hh, :HEAD_DIM, :] / acc_ref[hh, HEAD_DIM:HEAD_DIM + 1, :] for hh in range(2)],
                         axis=0)
    o_ref[...] = ot.T.astype(BF16)


def flash(qa, ka, vt, batch, seq):
    t = qa.shape[0]
    tk = vt.shape[-1]
    tq = 4 * tk
    nq = seq // tq
    pairs = N_HEADS // 2
    return pl.pallas_call(
        functools.partial(_flash_kernel, tq=tq, tk=tk),
        grid=(batch, pairs, nq),
        in_specs=[pl.BlockSpec((tq, 2 * HEAD_LANES), lambda b, p, q: (b * nq + q, p)),
                  pl.BlockSpec((seq, 2 * HEAD_LANES), lambda b, p, q: (b, p)),
                  pl.BlockSpec((None, None, seq // tk, 2 * V_ROWS, tk), lambda b, p, q: (b, p, 0, 0, 0))],
        out_specs=pl.BlockSpec((tq, LANES), lambda b, p, q: (b * nq + q, p)),
        out_shape=jax.ShapeDtypeStruct((t, pairs * LANES), BF16),
        scratch_shapes=([pltpu.VMEM((2, tk, tq), F32)] * 4 + [pltpu.VMEM((2, tk, tq), BF16)] * 4
                        + [pltpu.VMEM((2, 1, tq), F32)] * 4
                        + [pltpu.VMEM((2, 1, tq), F32), pltpu.VMEM((2, V_ROWS, tq), F32)]),
        compiler_params=_params("arbitrary", "arbitrary", "arbitrary"),
        name="flash",
    )(qa, ka, vt)


def _ffn_kernel(x_ref, o_ref, wo_ref, g_ref, wg_ref, wu_ref, wd_ref, out_ref, hb_ref):
    f = pl.program_id(1)

    @pl.when(f == 0)
    def _():
        x1 = x_ref[...] + _dot(o_ref[...], wo_ref[...])
        out_ref[...] = x1
        hb_ref[...] = _rms(x1, g_ref[...]).astype(BF16)

    hb = hb_ref[...]
    a = _dot(hb, wg_ref[...])
    u = _dot(hb, wu_ref[...])
    act = (a * _sigmoid(a) * u).astype(BF16)
    out_ref[...] += _dot(act, wd_ref[...])


def ffn_dense(x, o, wo, g, wg, wu, wd, tm=1024, tf=512):
    t, d = x.shape
    ff = wg.shape[1]
    tm = min(tm, t)
    tf = min(tf, ff)
    return pl.pallas_call(
        _ffn_kernel,
        grid=(t // tm, ff // tf),
        in_specs=[pl.BlockSpec((tm, d), lambda i, f: (i, 0)), pl.BlockSpec((tm, d), lambda i, f: (i, 0)),
                  pl.BlockSpec((d, d), lambda i, f: (0, 0)), pl.BlockSpec((1, d), lambda i, f: (0, 0)),
                  pl.BlockSpec((d, tf), lambda i, f: (0, f)), pl.BlockSpec((d, tf), lambda i, f: (0, f)),
                  pl.BlockSpec((tf, d), lambda i, f: (f, 0))],
        out_specs=pl.BlockSpec((tm, d), lambda i, f: (i, 0)),
        out_shape=jax.ShapeDtypeStruct((t, d), F32),
        scratch_shapes=[pltpu.VMEM((tm, d), BF16)],
        compiler_params=_params("arbitrary", "arbitrary"),
        name="ffn_dense",
    )(x, o, wo.astype(BF16), g.reshape(1, d), wg.astype(BF16), wu.astype(BF16), wd.astype(BF16))


def _conv_pre_kernel(x_ref, g_ref, w1_ref, b1_ref, u_ref):
    tm, d = x_ref.shape
    hb = _rms(x_ref[...], g_ref[...]).astype(BF16)
    cw = 256
    for cc in range(d // cw):
        lo, hi = cc * cw, (cc + 1) * cw
        a = _dot(hb, w1_ref[:, lo:hi]) + b1_ref[:, lo:hi]
        b = _dot(hb, w1_ref[:, d + lo:d + hi]) + b1_ref[:, d + lo:d + hi]
        u = a * _sigmoid(b)
        for kk in range(cw // LANES):
            k = cc * (cw // LANES) + kk
            u_ref[k * tm:(k + 1) * tm, :] = u[:, kk * LANES:(kk + 1) * LANES]


def conv_pre(x, g, w1, b1, tm):
    t, d = x.shape
    nl = d // LANES
    return pl.pallas_call(
        _conv_pre_kernel,
        grid=(t // tm,),
        in_specs=[pl.BlockSpec((tm, d), lambda i: (i, 0)), pl.BlockSpec((1, d), lambda i: (0, 0)),
                  pl.BlockSpec((d, 2 * d), lambda i: (0, 0)), pl.BlockSpec((1, 2 * d), lambda i: (0, 0))],
        out_specs=pl.BlockSpec((nl * tm, LANES), lambda i: (i, 0)),
        out_shape=jax.ShapeDtypeStruct((t * nl, LANES), F32),
        compiler_params=_params("arbitrary"),
        name="conv_pre",
    )(x, g.reshape(1, d), w1.astype(BF16), b1.reshape(1, 2 * d))


def _conv_post_kernel(x_ref, u_ref, wdw_ref, bdw_ref, lng_ref, lnb_ref, w2_ref, b2_ref,
                      out_ref, time_ref, ystack_ref, *, tiles_per_seq):
    i = pl.program_id(0)
    tm, d = x_ref.shape
    nl = d // LANES
    group = 16
    n_groups = tm // group

    @pl.when(i % tiles_per_seq == 0)
    def _():
        time_ref[0:CONV_HALO] = jnp.zeros((CONV_HALO, nl, LANES), F32)

    def to_time_major(r, carry):
        time_ref[CONV_HALO + r] = u_ref[pl.ds(r, nl, stride=tm), :]
        return carry

    lax.fori_loop(0, tm, to_time_major, 0, unroll=8)

    shift = CONV_HALO - (CONV_WIDTH - 1)

    def conv(g, carry):
        base = g * group
        accs = [bdw_ref[...]] * group
        for j in range(CONV_WIDTH):
            w = wdw_ref[j]
            accs = [accs[tt] + w * time_ref[base + shift + j + tt] for tt in range(group)]
        for tt in range(group):
            ystack_ref[pl.ds(base + tt, nl, stride=tm), :] = accs[tt]
        return carry

    lax.fori_loop(0, n_groups, conv, 0)
    time_ref[0:CONV_HALO] = time_ref[tm:tm + CONV_HALO]

    y = jnp.concatenate([ystack_ref[k * tm:(k + 1) * tm, :] for k in range(nl)], axis=1)
    mu = jnp.mean(y, axis=-1, keepdims=True)
    yc = y - mu
    z = yc * lax.rsqrt(jnp.mean(yc * yc, axis=-1, keepdims=True) + EPS) * lng_ref[...] + lnb_ref[...]
    zb = (z * _sigmoid(z)).astype(BF16)
    out_ref[...] = x_ref[...] + _dot(zb, w2_ref[...]) + b2_ref[...]


def conv_post(x, u, wdw, bdw, lng, lnb, w2, b2, seq, tm):
    t, d = x.shape
    nl = d // LANES
    row = lambda a: a.reshape(1, d)
    return pl.pallas_call(
        functools.partial(_conv_post_kernel, tiles_per_seq=seq // tm),
        grid=(t // tm,),
        in_specs=[pl.BlockSpec((tm, d), lambda i: (i, 0)), pl.BlockSpec((nl * tm, LANES), lambda i: (i, 0)),
                  pl.BlockSpec((CONV_WIDTH, nl, LANES), lambda i: (0, 0, 0)), pl.BlockSpec((nl, LANES), lambda i: (0, 0)),
                  pl.BlockSpec((1, d), lambda i: (0, 0)), pl.BlockSpec((1, d), lambda i: (0, 0)),
                  pl.BlockSpec((d, d), lambda i: (0, 0)), pl.BlockSpec((1, d), lambda i: (0, 0))],
        out_specs=pl.BlockSpec((tm, d), lambda i: (i, 0)),
        out_shape=jax.ShapeDtypeStruct((t, d), F32),
        scratch_shapes=[pltpu.VMEM((tm + CONV_HALO, nl, LANES), F32), pltpu.VMEM((nl * tm, LANES), F32)],
        compiler_params=_params("arbitrary"),
        name="conv_post",
    )(x, u, wdw.reshape(CONV_WIDTH, nl, LANES), bdw.reshape(nl, LANES), row(lng), row(lnb), w2.astype(BF16), row(b2))


def _router_kernel(x_ref, g_ref, whi_ref, wlo_ref, slots_ref, gates_ref, counts_ref, carry_ref):
    i = pl.program_id(0)
    tm = x_ref.shape[0]

    @pl.when(i == 0)
    def _():
        carry_ref[...] = jnp.zeros_like(carry_ref)

    h = _rms(x_ref[...], g_ref[...])
    h_hi = h.astype(BF16)
    h_lo = (h - h_hi.astype(F32)).astype(BF16)
    logits = _dot(h_hi, whi_ref[...]) + _dot(h_lo, whi_ref[...]) + _dot(h_hi, wlo_ref[...])
    lane = lax.broadcasted_iota(I32, (tm, LANES), 1)
    lg = jnp.where(lane < N_EXPERTS, logits, -jnp.inf)
    m1 = jnp.max(lg, axis=1, keepdims=True)
    i1 = jnp.min(jnp.where(lg == m1, lane, LANES), axis=1, keepdims=True)
    lg2 = jnp.where(lane == i1, -jnp.inf, lg)
    m2 = jnp.max(lg2, axis=1, keepdims=True)
    i2 = jnp.min(jnp.where(lg2 == m2, lane, LANES), axis=1, keepdims=True)
    e2 = jnp.exp(m2 - m1)
    g0 = 1.0 / (1.0 + e2)
    g1 = e2 / (1.0 + e2)

    oh0 = lane == i1
    oh1 = lane == i2
    member = jnp.where(oh0 | oh1, 1.0, 0.0)
    row = lax.broadcasted_iota(I32, (tm, tm), 0)
    col = lax.broadcasted_iota(I32, (tm, tm), 1)
    before = jnp.where(col < row, 1.0, 0.0).astype(BF16)
    rank = _dot(before, member.astype(BF16)) + carry_ref[...]
    r0 = jnp.sum(jnp.where(oh0, rank, 0.0), axis=1, keepdims=True).astype(I32)
    r1 = jnp.sum(jnp.where(oh1, rank, 0.0), axis=1, keepdims=True).astype(I32)
    carry_ref[...] += jnp.sum(member, axis=0, keepdims=True)
    counts_ref[...] = carry_ref[...]
    slots_ref[...] = jnp.where(lane == 0, r0, jnp.where(lane == 1, r1, jnp.where(lane == 2, i1, jnp.where(lane == 3, i2, 0))))
    gates_ref[...] = jnp.where(lane == 0, g0, jnp.where(lane == 1, g1, 0.0))


def router(x, g, wr, tm=512):
    t, d = x.shape
    tm = min(tm, t)
    wr_p = jnp.zeros((d, LANES), F32).at[:, :N_EXPERTS].set(wr)
    w_hi = wr_p.astype(BF16)
    w_lo = (wr_p - w_hi.astype(F32)).astype(BF16)
    return pl.pallas_call(
        _router_kernel,
        grid=(t // tm,),
        in_specs=[pl.BlockSpec((tm, d), lambda i: (i, 0)), pl.BlockSpec((1, d), lambda i: (0, 0)),
                  pl.BlockSpec((d, LANES), lambda i: (0, 0)), pl.BlockSpec((d, LANES), lambda i: (0, 0))],
        out_specs=[pl.BlockSpec((tm, LANES), lambda i: (i, 0)), pl.BlockSpec((tm, LANES), lambda i: (i, 0)),
                   pl.BlockSpec((1, LANES), lambda i: (0, 0))],
        out_shape=[jax.ShapeDtypeStruct((t, LANES), I32), jax.ShapeDtypeStruct((t, LANES), F32),
                   jax.ShapeDtypeStruct((1, LANES), F32)],
        scratch_shapes=[pltpu.VMEM((1, LANES), F32)],
        compiler_params=_params("arbitrary"),
        name="router",
    )(x, g.reshape(1, d), w_hi, w_lo)


def _dispatch_kernel(fill_ref, slots_ref, x_ref, xs_ref, zero_ref, sem, *, tmf):
    i = pl.program_id(0)
    tm, d = x_ref.shape

    def body(g, carry):
        for sub in range(8):
            r = pl.multiple_of(g * 8, 8) + sub
            for k in range(2):
                s = slots_ref[0, 0, 2 * r + k]
                pltpu.make_async_copy(x_ref.at[pl.ds(r, 1), :], xs_ref.at[pl.ds(s, 1), :], sem).start()
        return carry

    lax.fori_loop(0, tm // 8, body, 0)
    for k in range(2):
        pltpu.make_async_copy(x_ref, xs_ref.at[pl.ds(0, tm), :], sem).wait()

    @pl.when(i == pl.num_programs(0) - 1)
    def _():
        zero_ref[...] = jnp.zeros_like(zero_ref)
        sizes = []
        size = tmf // 2
        while size >= 8:
            sizes.append(size)
            size //= 2

        def zero_row(r, carry):
            row_copy = pltpu.make_async_copy(zero_ref.at[pl.ds(0, 1), :], xs_ref.at[pl.ds(r, 1), :], sem)
            row_copy.start()
            row_copy.wait()
            return carry

        def zero_tile(n, carry):
            tile_copy = pltpu.make_async_copy(zero_ref, xs_ref.at[pl.ds(pl.multiple_of(n * tmf, tmf), tmf), :], sem)
            tile_copy.start()
            tile_copy.wait()
            return carry

        for e in range(N_EXPERTS):
            end = fill_ref[e]
            end8 = (end + 7) // 8 * 8
            lax.fori_loop(end, end8, zero_row, 0)
            rem = fill_ref[N_EXPERTS + e] - end8
            pos = end8
            for size in sizes:
                chunk = pltpu.make_async_copy(zero_ref.at[pl.ds(0, size), :],
                                              xs_ref.at[pl.ds(pl.multiple_of(pos, 8), size), :], sem)
                take = (rem & size) != 0

                @pl.when(take)
                def _():
                    chunk.start()
                    chunk.wait()

                pos = pos + jnp.where(take, size, 0)
        lax.fori_loop(fill_ref[2 * N_EXPERTS], fill_ref[2 * N_EXPERTS + 1], zero_tile, 0)


def dispatch(x, slots_flat, fill, rows, tmf, tm=512):
    t, d = x.shape
    tm = min(tm, t)
    nt = t // tm
    return pl.pallas_call(
        functools.partial(_dispatch_kernel, tmf=tmf),
        grid_spec=pltpu.PrefetchScalarGridSpec(
            num_scalar_prefetch=1,
            grid=(nt,),
            in_specs=[pl.BlockSpec((1, 1, 2 * tm), lambda i, c: (i, 0, 0), memory_space=pltpu.SMEM),
                      pl.BlockSpec((tm, d), lambda i, c: (i, 0))],
            out_specs=pl.BlockSpec(memory_space=pl.ANY),
            scratch_shapes=[pltpu.VMEM((tmf, d), F32), pltpu.SemaphoreType.DMA(())],
        ),
        out_shape=jax.ShapeDtypeStruct((rows, d), F32),
        compiler_params=_params("arbitrary"),
        name="dispatch",
    )(fill, slots_flat.reshape(nt, 1, 2 * tm), x)


def _moe_ffn_kernel(te_ref, na_ref, xs_ref, g_ref, wg_ref, wu_ref, wd_ref, ys_ref, hb_ref):
    i = pl.program_id(0)
    f = pl.program_id(1)

    @pl.when(f == 0)
    def _():
        ys_ref[...] = jnp.zeros_like(ys_ref)

    @pl.when(i < na_ref[0])
    def _():
        @pl.when(f == 0)
        def _():
            hb_ref[...] = _rms(xs_ref[...], g_ref[...]).astype(BF16)

        hb = hb_ref[...]
        a = _dot(hb, wg_ref[...])
        u = _dot(hb, wu_ref[...])
        act = (a * _sigmoid(a) * u).astype(BF16)
        ys_ref[...] += _dot(act, wd_ref[...])


def moe_ffn(xs, g, wg, wu, wd, tile_e, n_act, tmf, tf=512):
    rows, d = xs.shape
    ff = wg.shape[2]
    tf = min(tf, ff)
    nf = ff // tf

    def live(i, na):
        return jnp.minimum(i, na[0] - 1)

    def fidx(i, f, na):
        return jnp.where(i < na[0], f, nf - 1)

    return pl.pallas_call(
        _moe_ffn_kernel,
        grid_spec=pltpu.PrefetchScalarGridSpec(
            num_scalar_prefetch=2,
            grid=(rows // tmf, nf),
            in_specs=[pl.BlockSpec((tmf, d), lambda i, f, te, na: (live(i, na), 0)),
                      pl.BlockSpec((1, d), lambda i, f, te, na: (0, 0)),
                      pl.BlockSpec((None, d, tf), lambda i, f, te, na: (te[live(i, na)], 0, fidx(i, f, na))),
                      pl.BlockSpec((None, d, tf), lambda i, f, te, na: (te[live(i, na)], 0, fidx(i, f, na))),
                      pl.BlockSpec((None, tf, d), lambda i, f, te, na: (te[live(i, na)], fidx(i, f, na), 0))],
            out_specs=pl.BlockSpec((tmf, d), lambda i, f, te, na: (i, 0)),
            scratch_shapes=[pltpu.VMEM((tmf, d), BF16)],
        ),
        out_shape=jax.ShapeDtypeStruct((rows, d), F32),
        compiler_params=_params("arbitrary", "arbitrary"),
        name="moe_ffn",
    )(tile_e, n_act, xs, g.reshape(1, d), wg.astype(BF16), wu.astype(BF16), wd.astype(BF16))


def _combine_kernel(slots_ref, x_ref, gates_ref, ys_ref, gf_ref, out_ref, ybuf_ref, sem, *, final_norm):
    tm, d = x_ref.shape

    def body(g, carry):
        for sub in range(8):
            r = pl.multiple_of(g * 8, 8) + sub
            for k in range(2):
                s = slots_ref[0, 0, 2 * r + k]
                pltpu.make_async_copy(ys_ref.at[pl.ds(s, 1), :], ybuf_ref.at[k, pl.ds(r, 1), :], sem).start()
        return carry

    lax.fori_loop(0, tm // 8, body, 0)
    for k in range(2):
        pltpu.make_async_copy(ys_ref.at[pl.ds(0, tm), :], ybuf_ref.at[k], sem).wait()

    gates = gates_ref[...]
    out = x_ref[...] + gates[:, 0:1] * ybuf_ref[0] + gates[:, 1:2] * ybuf_ref[1]
    if final_norm:
        out = _rms(out, gf_ref[...])
    out_ref[...] = out


def combine(x, slots_flat, gates, ys, g_final, final_norm, tm=512):
    t, d = x.shape
    tm = min(tm, t)
    nt = t // tm
    return pl.pallas_call(
        functools.partial(_combine_kernel, final_norm=final_norm),
        grid=(nt,),
        in_specs=[pl.BlockSpec((1, 1, 2 * tm), lambda i: (i, 0, 0), memory_space=pltpu.SMEM),
                  pl.BlockSpec((tm, d), lambda i: (i, 0)), pl.BlockSpec((tm, LANES), lambda i: (i, 0)),
                  pl.BlockSpec(memory_space=pl.ANY), pl.BlockSpec((1, d), lambda i: (0, 0))],
        out_specs=pl.BlockSpec((tm, d), lambda i: (i, 0)),
        out_shape=jax.ShapeDtypeStruct((t, d), F32),
        scratch_shapes=[pltpu.VMEM((2, tm, d), F32), pltpu.SemaphoreType.DMA(())],
        compiler_params=_params("arbitrary"),
        name="combine",
    )(slots_flat.reshape(nt, 1, 2 * tm), x, gates, ys, g_final.reshape(1, d))


def moe_layer(x, g, wr, wg, wu, wd, g_final, final_norm, tmf=1024):
    t, d = x.shape
    tmf = min(tmf, t)
    meta, gates, counts = router(x, g, wr)
    cnt = counts[0, :N_EXPERTS].astype(I32)

    tiles_e = (cnt + tmf - 1) // tmf
    ends = jnp.cumsum(tiles_e)
    base = (ends - tiles_e) * tmf
    n_max = (2 * t) // tmf + N_EXPERTS
    slots_flat = (base[meta[:, 2:4]] + meta[:, 0:2]).reshape(-1)
    tile_e = jnp.minimum(jnp.sum((jnp.arange(n_max, dtype=I32)[:, None] >= ends[None, :]).astype(I32), axis=1),
                         N_EXPERTS - 1)
    n_act = ends[-1:].astype(I32)
    fill = jnp.concatenate([base + cnt, ends * tmf, n_act, jnp.full((1,), n_max, I32)]).astype(I32)

    xs = dispatch(x, slots_flat, fill, n_max * tmf, tmf)
    ys = moe_ffn(xs, g, wg, wu, wd, tile_e, n_act, tmf)
    return combine(x, slots_flat, gates, ys, g_final, final_norm)


def kernel(x, attn_norm, w_qkv, w_fgate, b_fgate, w_attn_out, conv_norm, w_pw1, b_pw1, w_dw, b_dw,
           conv_ln_g, conv_ln_b, w_pw2, b_pw2, dense_norm, w_dense_gate, w_dense_up, w_dense_down,
           moe_norm, w_router, w_exp_gate, w_exp_up, w_exp_down, final_norm):
    batch, seq, d = x.shape
    depth = attn_norm.shape[0] + conv_norm.shape[0]
    x = x.reshape(batch * seq, d)
    for layer in range(depth):
        j = layer // 2
        if layer % 2 == 0:
            qa, ka, vt = attn_pre(x, attn_norm[j], w_qkv[j], w_fgate[j], b_fgate[j], batch, seq)
            o = flash(qa, ka, vt, batch, seq)
            x = ffn_dense(x, o, w_attn_out[j], dense_norm[j], w_dense_gate[j], w_dense_up[j], w_dense_down[j])
        else:
            u = conv_pre(x, conv_norm[j], w_pw1[j], b_pw1[j], CONV_TILE)
            x = conv_post(x, u, w_dw[j], b_dw[j], conv_ln_g[j], conv_ln_b[j], w_pw2[j], b_pw2[j], seq, CONV_TILE)
            x = moe_layer(x, moe_norm[j], w_router[j], w_exp_gate[j], w_exp_up[j], w_exp_down[j],
                          final_norm, final_norm=(layer == depth - 1))
    return x.reshape(batch, seq, d)
```
